```python
import math
import jax, jax.numpy as jnp
from jax import lax
import numpy as np

D_MODEL = 1024
BATCH = 16
SEQ = 4096
DEPTH = 4

HEAD_DIM = 64
N_BRANCHES = 4
BRANCH_WIDTH = 256

RWKV_HEADS = 4
RWKV_WIDTH = RWKV_HEADS * HEAD_DIM
RWKV_DECAY_RANK = 64
RWKV_ICL_RANK = 64
RWKV_GATE_RANK = 128
RWKV_DECAY_SCALE = 0.606531
RWKV_LN_EPS = 64e-5
RWKV_COLS = 3 * RWKV_WIDTH + RWKV_DECAY_RANK + RWKV_ICL_RANK + RWKV_GATE_RANK

DIL_PATTERNS = ((128, 1), (512, 4), (2048, 16))
DIL_GROUPS = 3
DIL_HEADS_PER_GROUP = 4
DIL_HEADS = DIL_GROUPS * DIL_HEADS_PER_GROUP
DIL_QKV = DIL_HEADS * HEAD_DIM
DIL_COLS = 3 * DIL_QKV

DIFF_HEADS = 4
DIFF_QK_DIM = 32
DIFF_V_DIM = 2 * DIFF_QK_DIM
DIFF_Q_COLS = DIFF_HEADS * 2 * DIFF_QK_DIM
DIFF_V_COLS = DIFF_HEADS * DIFF_V_DIM
DIFF_COLS = 2 * DIFF_Q_COLS + DIFF_V_COLS
DIFF_QBLOCK = 128
DIFF_SUBLN_EPS = 1e-5

CONV_CHANNELS = 256
CONV_WIDTH = 31
CONV_COLS = 2 * CONV_CHANNELS
CONV_LN_EPS = 1e-5

GATE_COLS = N_BRANCHES * D_MODEL
IN_COLS = RWKV_COLS + DIL_COLS + DIFF_COLS + CONV_COLS + GATE_COLS

D_FF = ((8 * D_MODEL + 3 * 256 - 1) // (3 * 256)) * 256

NORM_EPS = 1e-6
NEG_INF = -1e30

kernel_name = 'hybrid_rwkv7_dilated_diffattn_conformer_block'


def _split(t, sizes):
    out, start = [], 0
    for n in sizes:
        out.append(t[..., start:start + n])
        start += n
    return out


def rms_norm(x, gain, eps=NORM_EPS):
    xf = x.astype(jnp.float32)
    y = xf * lax.rsqrt(jnp.mean(xf * xf, axis=-1, keepdims=True) + eps)
    return (y * gain.astype(jnp.float32)).astype(x.dtype)


def layer_norm_f32(x, gain, bias, eps):
    xf = x.astype(jnp.float32)
    mu = jnp.mean(xf, axis=-1, keepdims=True)
    var = jnp.mean(jnp.square(xf - mu), axis=-1, keepdims=True)
    return (xf - mu) * lax.rsqrt(var + eps) * gain.astype(jnp.float32) + bias.astype(jnp.float32)


def alibi_slopes(n):
    return jnp.asarray(2.0 ** (-8.0 * np.arange(1, n + 1) / n), dtype=jnp.float32)


def rwkv7_time_mix(p, mu, w0, w_up, a0, a_up, g_up, k_k, k_a, r_k, ln_g, ln_b):
    B, S, _ = p.shape
    H, N = RWKV_HEADS, HEAD_DIM
    f32 = jnp.float32
    p_prev = jnp.pad(p[:, :-1], ((0, 0), (1, 0), (0, 0)))
    p = p + (p_prev - p) * mu
    r, k, v, zw, za, zg = _split(p, (RWKV_WIDTH, RWKV_WIDTH, RWKV_WIDTH, RWKV_DECAY_RANK, RWKV_ICL_RANK, RWKV_GATE_RANK))
    log_w = -RWKV_DECAY_SCALE * jax.nn.sigmoid((w0 + jnp.tanh(zw) @ w_up).astype(f32))
    a = jax.nn.sigmoid((a0 + za @ a_up).astype(f32))
    g = (jax.nn.sigmoid(zg) @ g_up).astype(f32)
    heads = lambda t: t.astype(f32).reshape(B, S, H, N)
    r, k, v, log_w, a = heads(r), heads(k), heads(v), heads(log_w), heads(a)
    kk = k * k_k.astype(f32).reshape(H, N)
    kk = kk / jnp.maximum(jnp.sqrt(jnp.sum(kk * kk, axis=-1, keepdims=True)), 1e-12)
    k = k * (1.0 + (a - 1.0) * k_a.astype(f32).reshape(H, N))
    w = jnp.exp(log_w)

    def step(state, inp):
        r_t, w_t, k_t, v_t, kk_t, a_t = inp
        sa = jnp.einsum('bhvk,bhk->bhv', state, -kk_t)
        state = (state * w_t[:, :, None, :]
                 + sa[..., None] * (kk_t * a_t)[:, :, None, :]
                 + v_t[..., None] * k_t[:, :, None, :])
        return state, jnp.einsum('bhvk,bhk->bhv', state, r_t)

    xs = tuple(t.transpose(1, 0, 2, 3) for t in (r, w, k, v, kk, a))
    _, o = lax.scan(step, jnp.zeros((B, H, N, N), f32), xs)
    o = o.transpose(1, 0, 2, 3)
    o = layer_norm_f32(o, ln_g.reshape(H, N), ln_b.reshape(H, N), RWKV_LN_EPS)
    o = o + jnp.sum(r * k * r_k.astype(f32), axis=-1, keepdims=True) * v
    return (o.reshape(B, S, H * N) * g).astype(p.dtype)


def _dilated_group(q, k, v, window, dilation, slopes):
    B, S, H, Dh = q.shape
    n = window // dilation
    L = S // dilation
    nb = -(-L // n)
    Lp = nb * n

    def to_sub(t):
        t = t.reshape(B, L, dilation, H, Dh).transpose(0, 2, 1, 3, 4)
        return jnp.pad(t, ((0, 0), (0, 0), (0, Lp - L), (0, 0), (0, 0)))

    def band(t):
        t = jnp.pad(t, ((0, 0), (0, 0), (n, 0), (0, 0), (0, 0))).reshape(B, dilation, nb + 1, n, H, Dh)
        return jnp.concatenate([t[:, :, :-1], t[:, :, 1:]], axis=3)

    qb = to_sub(q).reshape(B, dilation, nb, n, H, Dh)
    kb, vb = band(to_sub(k)), band(to_sub(v))
    s = jnp.einsum('brcqhd,brckhd->brchqk', qb, kb, preferred_element_type=jnp.float32) * (Dh ** -0.5)
    kj = jnp.arange(2 * n)
    rel = n + jnp.arange(n)[:, None] - kj[None, :]
    front_pad = (jnp.arange(nb)[:, None, None] == 0) & (kj[None, None, :] < n)
    valid = ((rel >= 0) & (rel <= n))[None] & ~front_pad
    bias = -slopes[:, None, None] * (dilation * rel).astype(jnp.float32)[None]
    s = jnp.where(valid[None, None, :, None], s + bias[None, None, None], NEG_INF)
    lse = jax.nn.logsumexp(s, axis=-1)
    pr = jnp.exp(s - lse[..., None])
    o = jnp.einsum('brchqk,brckhd->brcqhd', pr, vb.astype(jnp.float32))
    o = o.reshape(B, dilation, Lp, H, Dh)[:, :, :L].transpose(0, 2, 1, 3, 4).reshape(B, S, H, Dh)
    lse = lse.transpose(0, 1, 2, 4, 3).reshape(B, dilation, Lp, H)[:, :, :L].transpose(0, 2, 1, 3).reshape(B, S, H)
    return o, lse


def dilated_attention(q, k, v, slopes):
    outs, lses = [], []
    for gi, (window, dilation) in enumerate(DIL_PATTERNS):
        o, lse = _dilated_group(q[:, :, gi], k[:, :, gi], v[:, :, gi], window, dilation, slopes[gi])
        outs.append(o)
        lses.append(lse)
    wts = jax.nn.softmax(jnp.stack(lses, axis=0), axis=0)
    return jnp.sum(wts[..., None] * jnp.stack(outs, axis=0), axis=0)


def diff_attention(q, k, v, lam, slopes, subln_g, lambda_init):
    B, S, H, _, dq = q.shape
    nq = S // DIFF_QBLOCK
    qb = q.reshape(B, nq, DIFF_QBLOCK, H, 2, dq).transpose(1, 0, 2, 3, 4, 5)
    kpos = jnp.arange(S)
    vf = v.astype(jnp.float32)

    def block(args):
        c, q_blk = args
        s = jnp.einsum('bqhmd,bkhmd->bhmqk', q_blk, k, preferred_element_type=jnp.float32) * (dq ** -0.5)
        rel = (c * DIFF_QBLOCK + jnp.arange(DIFF_QBLOCK))[:, None] - kpos[None, :]
        s = s - slopes[:, None, None, None] * rel.astype(jnp.float32)
        s = jnp.where(rel >= 0, s, NEG_INF)
        pr = jax.nn.softmax(s, axis=-1)
        att = pr[:, :, 0] - lam * pr[:, :, 1]
        return jnp.einsum('bhqk,bkhd->bqhd', att, vf)

    o = lax.map(block, (jnp.arange(nq), qb))
    o = o.transpose(1, 0, 2, 3, 4).reshape(B, S, H, -1)
    o = o * lax.rsqrt(jnp.mean(o * o, axis=-1, keepdims=True) + DIFF_SUBLN_EPS) * subln_g.astype(jnp.float32)
    return (o * (1.0 - lambda_init)).reshape(B, S, -1).astype(q.dtype)


def conformer_conv(p, dw_w, dw_b, ln_g, ln_b):
    a, b = _split(p, (CONV_CHANNELS, CONV_CHANNELS))
    u = a * jax.nn.sigmoid(b)
    y = lax.conv_general_dilated(u, dw_w, window_strides=(1,), padding=((CONV_WIDTH - 1, 0),),
                                 dimension_numbers=('NWC', 'WIO', 'NWC'),
                                 feature_group_count=CONV_CHANNELS) + dw_b
    y = layer_norm_f32(y, ln_g, ln_b, CONV_LN_EPS)
    return jax.nn.silu(y).astype(p.dtype)


def setup_inputs(seed: int = 0) -> dict:
    key = jax.random.key(seed)
    ks = iter(jax.random.split(key, 40))
    L, D = DEPTH, D_MODEL
    nrm = lambda shape, scale: scale * jax.random.normal(next(ks), shape, jnp.float32)
    gain = lambda shape: 1.0 + nrm(shape, 0.02)
    return {
        'x': nrm((BATCH, SEQ, D), 1.0),
        'norm_mix_pre': gain((L, D)),
        'norm_mix_post': gain((L, D)),
        'norm_ffn_pre': gain((L, D)),
        'norm_ffn_post': gain((L, D)),
        'w_in': nrm((L, D, IN_COLS), D ** -0.5),
        'gate_bias': nrm((L, GATE_COLS), 0.01),
        'rwkv_mu': jax.random.uniform(next(ks), (L, RWKV_COLS), jnp.float32),
        'rwkv_w0': nrm((L, RWKV_WIDTH), 1.0),
        'rwkv_w_up': nrm((L, RWKV_DECAY_RANK, RWKV_WIDTH), 0.5 * RWKV_DECAY_RANK ** -0.5),
        'rwkv_a0': nrm((L, RWKV_WIDTH), 0.1),
        'rwkv_a_up': nrm((L, RWKV_ICL_RANK, RWKV_WIDTH), 0.5 * RWKV_ICL_RANK ** -0.5),
        'rwkv_g_up': nrm((L, RWKV_GATE_RANK, RWKV_WIDTH), RWKV_GATE_RANK ** -0.5),
        'rwkv_k_k': 0.85 + nrm((L, RWKV_WIDTH), 0.02),
        'rwkv_k_a': gain((L, RWKV_WIDTH)),
        'rwkv_r_k': nrm((L, RWKV_HEADS, HEAD_DIM), 0.1),
        'rwkv_ln_g': gain((L, RWKV_WIDTH)),
        'rwkv_ln_b': nrm((L, RWKV_WIDTH), 0.01),
        'diff_lam_q1': nrm((L, DIFF_QK_DIM), 0.1),
        'diff_lam_k1': nrm((L, DIFF_QK_DIM), 0.1),
        'diff_lam_q2': nrm((L, DIFF_QK_DIM), 0.1),
        'diff_lam_k2': nrm((L, DIFF_QK_DIM), 0.1),
        'diff_subln_g': gain((L, DIFF_V_DIM)),
        'conv_dw_w': nrm((L, CONV_WIDTH, 1, CONV_CHANNELS), CONV_WIDTH ** -0.5),
        'conv_dw_b': nrm((L, CONV_CHANNELS), 0.01),
        'conv_ln_g': gain((L, CONV_CHANNELS)),
        'conv_ln_b': nrm((L, CONV_CHANNELS), 0.01),
        'w_branch': nrm((L, N_BRANCHES, BRANCH_WIDTH, D), BRANCH_WIDTH ** -0.5),
        'w_out': nrm((L, D, D), D ** -0.5),
        'ffn_w_gate': nrm((L, D, D_FF), D ** -0.5),
        'ffn_w_up': nrm((L, D, D_FF), D ** -0.5),
        'ffn_w_down': nrm((L, D_FF, D), D_FF ** -0.5),
    }


def reference(x, norm_mix_pre, norm_mix_post, norm_ffn_pre, norm_ffn_post, w_in, gate_bias,
              rwkv_mu, rwkv_w0, rwkv_w_up, rwkv_a0, rwkv_a_up, rwkv_g_up, rwkv_k_k, rwkv_k_a,
              rwkv_r_k, rwkv_ln_g, rwkv_ln_b, diff_lam_q1, diff_lam_k1, diff_lam_q2, diff_lam_k2,
              diff_subln_g, conv_dw_w, conv_dw_b, conv_ln_g, conv_ln_b, w_branch, w_out,
              ffn_w_gate, ffn_w_up, ffn_w_down):
    B, S, D = x.shape
    dil_slopes = alibi_slopes(DIL_HEADS).reshape(DIL_GROUPS, DIL_HEADS_PER_GROUP)
    diff_slopes = alibi_slopes(DIFF_HEADS)
    for l in range(DEPTH):
        h = rms_norm(x, norm_mix_pre[l])
        proj = h @ w_in[l]
        p_rwkv, p_dil, p_diff, p_conv, p_gate = _split(proj, (RWKV_COLS, DIL_COLS, DIFF_COLS, CONV_COLS, GATE_COLS))

        y_a = rwkv7_time_mix(p_rwkv, rwkv_mu[l], rwkv_w0[l], rwkv_w_up[l], rwkv_a0[l], rwkv_a_up[l],
                             rwkv_g_up[l], rwkv_k_k[l], rwkv_k_a[l], rwkv_r_k[l], rwkv_ln_g[l], rwkv_ln_b[l])

        dq, dk, dv = _split(p_dil, (DIL_QKV, DIL_QKV, DIL_QKV))
        grp = lambda t: t.reshape(B, S, DIL_GROUPS, DIL_HEADS_PER_GROUP, HEAD_DIM)
        y_b = dilated_attention(grp(dq), grp(dk), grp(dv), dil_slopes).reshape(B, S, BRANCH_WIDTH).astype(x.dtype)

        cq, ck, cv = _split(p_diff, (DIFF_Q_COLS, DIFF_Q_COLS, DIFF_V_COLS))
        lambda_init = 0.8 - 0.6 * math.exp(-0.3 * l)
        lam = (jnp.exp(jnp.sum(diff_lam_q1[l].astype(jnp.float32) * diff_lam_k1[l].astype(jnp.float32)))
               - jnp.exp(jnp.sum(diff_lam_q2[l].astype(jnp.float32) * diff_lam_k2[l].astype(jnp.float32)))
               + lambda_init)
        y_c = diff_attention(cq.reshape(B, S, DIFF_HEADS, 2, DIFF_QK_DIM),
                             ck.reshape(B, S, DIFF_HEADS, 2, DIFF_QK_DIM),
                             cv.reshape(B, S, DIFF_HEADS, DIFF_V_DIM),
                             lam, diff_slopes, diff_subln_g[l], lambda_init)

        y_d = conformer_conv(p_conv, conv_dw_w[l], conv_dw_b[l], conv_ln_g[l], conv_ln_b[l])

        ys = jnp.stack([y_a, y_b, y_c, y_d], axis=2)
        gates = jax.nn.sigmoid(p_gate + gate_bias[l]).reshape(B, S, N_BRANCHES, D)
        merged = jnp.einsum('bsnd,bsnd->bsd', gates, jnp.einsum('bsnc,ncd->bsnd', ys, w_branch[l]))
        x = x + rms_norm(merged @ w_out[l], norm_mix_post[l])

        h = rms_norm(x, norm_ffn_pre[l])
        f = (jax.nn.silu(h @ ffn_w_gate[l]) * (h @ ffn_w_up[l])) @ ffn_w_down[l]
        x = x + rms_norm(f, norm_ffn_post[l])
    return x
```

```python
import functools
import math

import numpy as np
import jax
import jax.numpy as jnp
from jax import lax
from jax.experimental import pallas as pl
from jax.experimental.pallas import tpu as pltpu

F32 = jnp.float32
BF16 = jnp.bfloat16

D_MODEL = 1024
DEPTH = 4
HEAD_DIM = 64
BRANCH_WIDTH = 256
N_BRANCHES = 4

RWKV_HEADS = 4
RWKV_WIDTH = 256
RWKV_COLS = 1024
RWKV_DECAY_SCALE = 0.606531
RWKV_LN_EPS = 64e-5
RWKV_CHUNK = 64

DIL_PATTERNS = ((128, 1), (512, 4), (2048, 16))
DIL_GROUPS = 3
DIL_HEADS = 12
DIL_QKV = 768
DIL_COLS = 2304
DIL_BLOCK = 128

DIFF_HEADS = 4
DIFF_QK_DIM = 32
DIFF_COLS = 768
DIFF_TILE = 256
DIFF_SUBLN_EPS = 1e-5

CONV_CHANNELS = 256
CONV_WIDTH = 31
CONV_COLS = 512
CONV_LN_EPS = 1e-5
CONV_HALO = 32

MIX_COLS = RWKV_COLS + DIL_COLS + DIFF_COLS + CONV_COLS
D_FF = 2816
NORM_EPS = 1e-6
NEG_INF = -1e30

VMEM_LIMIT_BYTES = 56 * 1024 * 1024
TOKEN_TILE = 512
MXU_COLS = 256


def _dot(a, b):
    return jnp.dot(a, b, preferred_element_type=F32)


def _dot_nt(a, b):
    return lax.dot_general(a, b, (((1,), (1,)), ((), ())), preferred_element_type=F32)


def _dot_tn(a, b):
    return lax.dot_general(a, b, (((0,), (0,)), ((), ())), preferred_element_type=F32)


def _dot_split(x, m):
    hi = x.astype(BF16)
    lo = (x - hi.astype(F32)).astype(BF16)
    return _dot(hi, m) + _dot(lo, m)


def _dot_split_rhs(m, x):
    hi = x.astype(BF16)
    lo = (x - hi.astype(F32)).astype(BF16)
    return _dot(m, hi) + _dot(m, lo)


def _rms(x, gain):
    return x * lax.rsqrt(jnp.mean(x * x, axis=-1, keepdims=True) + NORM_EPS) * gain


def _sigmoid(x):
    return 1.0 / (1.0 + jnp.exp(-x))


def _const_spec(shape):
    nd = len(shape)
    return pl.BlockSpec(shape, lambda *_: (0,) * nd)


def _params(*sem):
    return pltpu.CompilerParams(dimension_semantics=sem, vmem_limit_bytes=VMEM_LIMIT_BYTES)


def _inproj_kernel(x_ref, g_ref, w_ref, o_rwkv, o_dil, o_diff, o_conv):
    h = _rms(x_ref[...], g_ref[...]).astype(BF16)
    col = 0
    for o_ref, width in ((o_rwkv, RWKV_COLS), (o_dil, DIL_COLS), (o_diff, DIFF_COLS), (o_conv, CONV_COLS)):
        for c0 in range(0, width, MXU_COLS):
            y = _dot(h, w_ref[:, col + c0:col + c0 + MXU_COLS])
            o_ref[:, c0:c0 + MXU_COLS] = y.astype(o_ref.dtype)
        col += width


def _inproj(x2, gain, w_mix):
    T = x2.shape[0]
    tm = TOKEN_TILE
    row = lambda w: pl.BlockSpec((tm, w), lambda i: (i, 0))
    return pl.pallas_call(
        _inproj_kernel,
        grid=(T // tm,),
        in_specs=[row(D_MODEL), _const_spec((1, D_MODEL)), _const_spec((D_MODEL, MIX_COLS))],
        out_specs=[row(RWKV_COLS), row(DIL_COLS), row(DIFF_COLS), row(CONV_COLS)],
        out_shape=[jax.ShapeDtypeStruct((T, RWKV_COLS), F32), jax.ShapeDtypeStruct((T, DIL_COLS), BF16),
                   jax.ShapeDtypeStruct((T, DIFF_COLS), BF16), jax.ShapeDtypeStruct((T, CONV_COLS), F32)],
        compiler_params=_params("parallel"),
        name="inproj",
    )(x2, gain, w_mix)


_MASK_STRICT, _MASK_INCL, _MASK_EYE, _MASK_LEVEL0 = 0, 1, 2, 3
_INV_LEVELS = (1, 2, 4, 8, 16, 32)


def _rwkv_masks():
    n = RWKV_HEADS * RWKV_CHUNK
    r = np.arange(n)[:, None]
    c = np.arange(n)[None, :]
    same = (r // RWKV_CHUNK) == (c // RWKV_CHUNK)
    out = [same & (r > c), same & (r >= c), r == c]
    for s in _INV_LEVELS:
        out.append((r // (2 * s) == c // (2 * s)) & (r % (2 * s) >= s) & (c % (2 * s) < s))
    head = (r // RWKV_CHUNK) == (c // HEAD_DIM)
    out.append(head)
    return np.stack(out).astype(np.float32)


_MASK_HEAD = 3 + len(_INV_LEVELS)


def _chunk_sum_mats(ts):
    r = np.arange(ts)[:, None]
    c = np.arange(ts)[None, :]
    same = (r // RWKV_CHUNK) == (c // RWKV_CHUNK)
    return np.stack([same & (c <= r), same]).astype(np.float32)


def _head_sum_mat(width, group):
    r = np.arange(width)[:, None]
    c = np.arange(width)[None, :]
    return ((r // group) == (c // group)).astype(np.float32)


def _rwkv_kernel(p_ref, mu_ref, w0_ref, a0_ref, wa_ref, gup_ref, kk_ref, ka_ref, rk_ref, lng_ref, lnb_ref,
                 masks_ref, csum_ref, hsum_ref, o_ref,
                 state_ref, prev_ref, at_ref, bt_ref, kt_ref, rt_ref, bh_ref, kh_ref, v_ref, ec_ref, oc_ref):
    ts = p_ref.shape[1]
    W = RWKV_WIDTH
    C = RWKV_CHUNK

    @pl.when(pl.program_id(1) == 0)
    def _():
        state_ref[...] = jnp.zeros_like(state_ref)
        prev_ref[...] = jnp.zeros_like(prev_ref)

    p = p_ref[0]
    row = lax.broadcasted_iota(jnp.int32, p.shape, 0)
    p_prev = jnp.where(row == 0, prev_ref[...], pltpu.roll(p, 1, 0))
    prev_ref[...] = p[ts - 1:ts, :]
    p = p + (p_prev - p) * mu_ref[...]

    r = p[:, 0:W]
    k = p[:, W:2 * W]
    v = p[:, 2 * W:3 * W]
    zwa = p[:, 3 * W:3 * W + 128]
    zg = p[:, 3 * W + 128:]
    lane = lax.broadcasted_iota(jnp.int32, zwa.shape, 1)
    tz = jnp.where(lane < 64, jnp.tanh(zwa), zwa)
    wa = _dot(tz.astype(BF16), wa_ref[...])
    logw = -RWKV_DECAY_SCALE * _sigmoid(w0_ref[...] + wa[:, :W])
    alpha = _sigmoid(a0_ref[...] + wa[:, W:])
    g = _dot(_sigmoid(zg).astype(BF16), gup_ref[...])

    hsum = hsum_ref[...]
    kk = k * kk_ref[...]
    kk = kk * lax.rsqrt(jnp.maximum(_dot_split(kk * kk, hsum), 1e-24))
    k2 = k * (1.0 + (alpha - 1.0) * ka_ref[...])
    bvec = kk * alpha

    cum = _dot_split_rhs(csum_ref[0], logw)
    cum_end = _dot_split_rhs(csum_ref[1], logw)
    e_neg = jnp.exp(-cum)
    e_rem = jnp.exp(cum_end - cum)
    at_ref[...] = -kk * jnp.exp(cum - logw)
    bt_ref[...] = bvec * e_neg
    kt_ref[...] = k2 * e_neg
    rt_ref[...] = r * jnp.exp(cum)
    bh_ref[...] = bvec * e_rem
    kh_ref[...] = k2 * e_rem
    v_ref[...] = v
    ec_ref[...] = jnp.exp(cum_end)

    def tile4(x):
        x4 = jnp.concatenate([x, x, x, x], axis=0)
        return jnp.where(masks_ref[_MASK_HEAD] > 0, x4, 0.0).astype(BF16)

    def keep(m, x):
        return jnp.where(masks_ref[m] > 0, x, 0.0)

    def chunk(c, carry):
        r0 = pl.multiple_of(c * C, C)
        rows = pl.ds(r0, C)
        at = tile4(at_ref[rows, :])
        rt = tile4(rt_ref[rows, :])
        bt = tile4(bt_ref[rows, :])
        kt = tile4(kt_ref[rows, :])
        vr = tile4(v_ref[rows, :])
        n = RWKV_HEADS * C
        a = _dot_nt(jnp.concatenate([at, rt], axis=0), jnp.concatenate([bt, kt], axis=0))
        a_ab = keep(_MASK_STRICT, a[:n, :n])
        a_ak = keep(_MASK_STRICT, a[:n, n:])
        a_rb = keep(_MASK_INCL, a[n:, :n])
        a_rk = keep(_MASK_INCL, a[n:, n:])
        t = masks_ref[_MASK_EYE] + keep(_MASK_LEVEL0, a_ab)
        for li in range(1, len(_INV_LEVELS)):
            x = keep(_MASK_LEVEL0 + li, a_ab).astype(BF16)
            tb = t.astype(BF16)
            t = t + _dot(tb, _dot(x, tb).astype(BF16))
        tb = t.astype(BF16)
        w_t = _dot(tb, at)
        u_t = _dot(tb, _dot(a_ak.astype(BF16), vr).astype(BF16))
        arkv = _dot(a_rk.astype(BF16), vr)

        s = state_ref[...]
        sb = s.astype(BF16)
        u = _dot_nt(w_t.astype(BF16), sb) + u_t
        ub = u.astype(BF16)
        o = _dot_nt(rt, sb) + _dot(a_rb.astype(BF16), ub) + arkv
        oc_ref[rows, :] = o[0:C] + o[C:2 * C] + o[2 * C:3 * C] + o[3 * C:4 * C]
        bh = tile4(bh_ref[rows, :])
        kh = tile4(kh_ref[rows, :])
        upd = _dot_tn(jnp.concatenate([ub, vr], axis=0), jnp.concatenate([bh, kh], axis=0))
        state_ref[...] = s * ec_ref[pl.ds(r0, 1), :] + upd
        return carry

    lax.fori_loop(0, ts // C, chunk, 0)

    o = oc_ref[...]
    hmean = hsum * (1.0 / HEAD_DIM)
    mean = _dot_split(o, hmean)
    d = o - mean
    var = _dot_split(d * d, hmean)
    o = d * lax.rsqrt(var + RWKV_LN_EPS) * lng_ref[...] + lnb_ref[...]
    bonus = _dot_split(r * k2 * rk_ref[...], hsum)
    o_ref[0] = ((o + bonus * v) * g).astype(o_ref.dtype)


def _rwkv(p, mu, w0, a0, wa_up, g_up, k_k, k_a, r_k, ln_g, ln_b):
    B, S, _ = p.shape
    ts = TOKEN_TILE
    W = RWKV_WIDTH
    n = RWKV_HEADS * RWKV_CHUNK
    masks = jnp.asarray(_rwkv_masks())
    csum = jnp.asarray(_chunk_sum_mats(ts), BF16)
    hsum = jnp.asarray(_head_sum_mat(W, HEAD_DIM), BF16)
    vec = _const_spec((1, W))
    buf = lambda: pltpu.VMEM((ts, W), F32)
    return pl.pallas_call(
        _rwkv_kernel,
        grid=(B, S // ts),
        in_specs=[pl.BlockSpec((1, ts, RWKV_COLS), lambda b, s: (b, s, 0)),
                  _const_spec((1, RWKV_COLS)), vec, vec, _const_spec((128, 2 * W)), _const_spec((128, W)),
                  vec, vec, vec, vec, vec,
                  _const_spec(masks.shape), _const_spec(csum.shape), _const_spec(hsum.shape)],
        out_specs=pl.BlockSpec((1, ts, W), lambda b, s: (b, s, 0)),
        out_shape=jax.ShapeDtypeStruct((B, S, W), BF16),
        scratch_shapes=[pltpu.VMEM((n, n), F32), pltpu.VMEM((1, RWKV_COLS), F32)] + [buf() for _ in range(9)],
        compiler_params=_params("parallel", "arbitrary"),
        name="rwkv7",
    )(p, mu, w0, a0, wa_up, g_up, k_k, k_a, r_k, ln_g, ln_b, masks, csum, hsum)


def _alibi_slopes(n):
    return 2.0 ** (-8.0 * np.arange(1, n + 1) / n)


def _dil_bias(group):
    window, dilation = DIL_PATTERNS[group]
    n = window // dilation
    slopes = _alibi_slopes(DIL_HEADS).reshape(DIL_GROUPS, -1)[group]
    rel = n + np.arange(n)[:, None] - np.arange(2 * n)[None, :]
    valid = (rel >= 0) & (rel <= n)
    bias = -slopes[:, None, None] * (dilation * rel).astype(np.float64)[None]
    return np.where(valid[None], bias, NEG_INF).astype(np.float32)


def _dil_kernel(q_ref, kp_ref, kc_ref, vp_ref, vc_ref, bias_ref, o_ref, lse_ref):
    n = DIL_BLOCK
    first = pl.program_id(2) == 0
    q = q_ref[0]
    k = jnp.concatenate([kp_ref[0], kc_ref[0]], axis=0)
    v = jnp.concatenate([vp_ref[0], vc_ref[0]], axis=0)
    head = lax.broadcasted_iota(jnp.int32, q.shape, 1) // HEAD_DIM
    front = jnp.logical_and(first, lax.broadcasted_iota(jnp.int32, (n, 2 * n), 1) < n)
    out = jnp.zeros(q.shape, F32)
    lse = jnp.zeros(q.shape, F32)
    for h in range(q.shape[1] // HEAD_DIM):
        qh = jnp.where(head == h, q, jnp.zeros_like(q))
        s = _dot_nt(qh, k) * (HEAD_DIM ** -0.5) + bias_ref[h]
        s = jnp.where(front, NEG_INF, s)
        m = jnp.max(s, axis=-1, keepdims=True)
        e = jnp.exp(s - m)
        l = jnp.sum(e, axis=-1, keepdims=True)
        pv = _dot(e.astype(BF16), v)
        out = jnp.where(head == h, pv / l, out)
        lse = jnp.where(head == h, m + jnp.log(l), lse)
    o_ref[0] = out
    lse_ref[0] = lse


def _dilated_group(p_dil, group):
    B, S, _ = p_dil.shape
    window, d = DIL_PATTERNS[group]
    n = window // d
    assert n == DIL_BLOCK
    L = S // d
    assert L % n == 0
    nb = L // n
    W = BRANCH_WIDTH
    blocks = DIL_COLS // W
    sub = p_dil.reshape(B, L, d * DIL_COLS)
    bias = jnp.asarray(_dil_bias(group))
    qoff, koff, voff = group, DIL_QKV // W + group, 2 * DIL_QKV // W + group
    cur = lambda off: pl.BlockSpec((1, n, W), lambda b, r, c: (b, c, r * blocks + off))
    prev = lambda off: pl.BlockSpec((1, n, W), lambda b, r, c: (b, jnp.maximum(c - 1, 0), r * blocks + off))
    out_spec = pl.BlockSpec((1, n, W), lambda b, r, c: (b, c, r))
    o, lse = pl.pallas_call(
        _dil_kernel,
        grid=(B, d, nb),
        in_specs=[cur(qoff), prev(koff), cur(koff), prev(voff), cur(voff), _const_spec(bias.shape)],
        out_specs=[out_spec, out_spec],
        out_shape=[jax.ShapeDtypeStruct((B, L, d * W), F32)] * 2,
        compiler_params=_params("parallel", "parallel", "parallel"),
        name=f"dilated{group}",
    )(sub, sub, sub, sub, sub, bias)
    return o.reshape(B, S, W), lse.reshape(B, S, W)


def _diff_bias():
    t = DIFF_TILE
    slopes = _alibi_slopes(DIFF_HEADS)
    rel = (np.arange(t)[:, None] - np.arange(t)[None, :]).astype(np.float64)
    off = -slopes[:, None, None] * rel[None]
    diag = np.where(rel[None] >= 0, off, NEG_INF)
    return np.stack([off, diag]).astype(np.float32)


def _diff_kernel(q_ref, k_ref, v_ref, bias_ref, lamv_ref, gain_ref, hsum_ref, o_ref, m_ref, l_ref, acc_ref,
                 *, lambda_init):
    t = DIFF_TILE
    n_maps = 2 * DIFF_HEADS
    qi = pl.program_id(1)
    q = q_ref[0]
    lane = lax.broadcasted_iota(jnp.int32, q.shape, 1)
    head = lane // HEAD_DIM
    qs = [jnp.where(lane // DIFF_QK_DIM == c, q, jnp.zeros_like(q)) for c in range(n_maps)]
    slopes = _alibi_slopes(DIFF_HEADS)

    m_ref[...] = jnp.full(m_ref.shape, NEG_INF, F32)
    l_ref[...] = jnp.zeros_like(l_ref)
    acc_ref[...] = jnp.zeros_like(acc_ref)

    def block(kb, diag):
        rows = pl.ds(pl.multiple_of(kb * t, t), t)
        k = k_ref[0, rows, :]
        v = v_ref[0, rows, :]
        shift = ((qi - kb) * t).astype(F32)
        for c in range(n_maps):
            h, mp = divmod(c, 2)
            s = _dot_nt(qs[c], k) * (DIFF_QK_DIM ** -0.5) + bias_ref[diag, h] - float(slopes[h]) * shift
            m_prev = m_ref[c]
            m_new = jnp.maximum(m_prev, jnp.max(s, axis=-1, keepdims=True))
            a = jnp.exp(m_prev - m_new)
            e = jnp.exp(s - jnp.concatenate([m_new, m_new], axis=1))
            l_ref[c] = a * l_ref[c] + jnp.sum(e, axis=-1, keepdims=True)
            m_ref[c] = m_new
            pv = _dot(e.astype(BF16), v)
            acc = acc_ref[mp]
            acc_ref[mp] = jnp.where(head == h, acc * jnp.concatenate([a, a], axis=1) + pv, acc)

    def body(kb, carry):
        block(kb, 0)
        return carry

    lax.fori_loop(0, qi, body, 0)
    block(qi, 1)

    lamv = lamv_ref[...]
    lam = (jnp.exp(jnp.sum(lamv[0:1] * lamv[1:2], axis=-1, keepdims=True))
           - jnp.exp(jnp.sum(lamv[2:3] * lamv[3:4], axis=-1, keepdims=True)) + lambda_init)
    inv = [jnp.zeros(q.shape, F32), jnp.zeros(q.shape, F32)]
    for c in range(n_maps):
        h, mp = divmod(c, 2)
        r = 1.0 / l_ref[c]
        inv[mp] = jnp.where(head == h, jnp.concatenate([r, r], axis=1), inv[mp])
    o = acc_ref[0] * inv[0] - lam * (acc_ref[1] * inv[1])
    ms = _dot_split(o * o, hsum_ref[...]) * (1.0 / HEAD_DIM)
    o = o * lax.rsqrt(ms + DIFF_SUBLN_EPS) * gain_ref[...]
    o_ref[0] = (o * (1.0 - lambda_init)).astype(o_ref.dtype)


def _diff_attention(p_diff, lamv, subln_g, lambda_init):
    B, S, _ = p_diff.shape
    t = DIFF_TILE
    W = BRANCH_WIDTH
    bias = jnp.asarray(_diff_bias())
    hsum = jnp.asarray(_head_sum_mat(W, HEAD_DIM), BF16)
    gain = jnp.tile(subln_g.reshape(1, HEAD_DIM), (1, DIFF_HEADS))
    full = lambda col: pl.BlockSpec((1, S, W), lambda b, i: (b, 0, col))
    return pl.pallas_call(
        functools.partial(_diff_kernel, lambda_init=lambda_init),
        grid=(B, S // t),
        in_specs=[pl.BlockSpec((1, t, W), lambda b, i: (b, i, 0)), full(1), full(2),
                  _const_spec(bias.shape), _const_spec(lamv.shape), _const_spec((1, W)), _const_spec((W, W))],
        out_specs=pl.BlockSpec((1, t, W), lambda b, i: (b, i, 0)),
        out_shape=jax.ShapeDtypeStruct((B, S, W), BF16),
        scratch_shapes=[pltpu.VMEM((2 * DIFF_HEADS, t, 128), F32), pltpu.VMEM((2 * DIFF_HEADS, t, 128), F32),
                        pltpu.VMEM((2, t, W), F32)],
        compiler_params=_params("parallel", "arbitrary"),
        name="diffattn",
    )(p_diff, p_diff, p_diff, bias, lamv, gain, hsum)


def _conv_kernel(p_ref, w_ref, b_ref, g_ref, beta_ref, o_ref, u_ref):
    ts = p_ref.shape[1]
    Cc = CONV_CHANNELS
    halo = CONV_HALO
    rows = 64

    @pl.when(pl.program_id(1) == 0)
    def _():
        u_ref[0:halo, :] = jnp.zeros((halo, Cc), F32)

    @pl.when(pl.program_id(1) > 0)
    def _():
        u_ref[0:halo, :] = u_ref[ts:ts + halo, :]

    p = p_ref[0]
    u_ref[halo:halo + ts, :] = p[:, :Cc] * _sigmoid(p[:, Cc:])

    def block(i, carry):
        r0 = pl.multiple_of(i * rows, rows)
        acc = jnp.zeros((rows, Cc), F32) + b_ref[...]
        win = u_ref[pl.ds(r0, rows + halo), :]
        first_tap = halo - CONV_WIDTH + 1
        for b in range(8):
            wb = win if b == 0 else pltpu.roll(win, rows + halo - b, 0)
            for a in range(halo // 8 + 1):
                j = 8 * a + b - first_tap
                if 0 <= j < CONV_WIDTH:
                    acc = acc + w_ref[j:j + 1, :] * wb[8 * a:8 * a + rows]
        mu = jnp.mean(acc, axis=-1, keepdims=True)
        d = acc - mu
        var = jnp.mean(d * d, axis=-1, keepdims=True)
        y = d * lax.rsqrt(var + CONV_LN_EPS) * g_ref[...] + beta_ref[...]
        o_ref[0, pl.ds(r0, rows), :] = (y * _sigmoid(y)).astype(o_ref.dtype)
        return carry

    lax.fori_loop(0, ts // rows, block, 0)


def _conformer_conv(p_conv, dw_w, dw_b, ln_g, ln_b):
    B, S, _ = p_conv.shape
    ts = TOKEN_TILE
    Cc = CONV_CHANNELS
    vec = _const_spec((1, Cc))
    return pl.pallas_call(
        _conv_kernel,
        grid=(B, S // ts),
        in_specs=[pl.BlockSpec((1, ts, CONV_COLS), lambda b, s: (b, s, 0)), _const_spec((CONV_WIDTH, Cc)),
                  vec, vec, vec],
        out_specs=pl.BlockSpec((1, ts, Cc), lambda b, s: (b, s, 0)),
        out_shape=jax.ShapeDtypeStruct((B, S, Cc), BF16),
        scratch_shapes=[pltpu.VMEM((ts + CONV_HALO, Cc), F32)],
        compiler_params=_params("parallel", "arbitrary"),
        name="conformer_conv",
    )(p_conv, dw_w, dw_b, ln_g, ln_b)


def _merge_kernel(x_ref, ya_ref, o0_ref, l0_ref, o1_ref, l1_ref, o2_ref, l2_ref, yc_ref, yd_ref,
                  gpre_ref, gpost_ref, wg_ref, gb_ref, wb_ref, wo_ref, out_ref):
    x = x_ref[...]
    h = _rms(x, gpre_ref[...]).astype(BF16)
    l0, l1, l2 = l0_ref[...], l1_ref[...], l2_ref[...]
    m = jnp.maximum(jnp.maximum(l0, l1), l2)
    e0, e1, e2 = jnp.exp(l0 - m), jnp.exp(l1 - m), jnp.exp(l2 - m)
    yb = (e0 * o0_ref[...] + e1 * o1_ref[...] + e2 * o2_ref[...]) / (e0 + e1 + e2)
    ys = (ya_ref[...], yb.astype(BF16), yc_ref[...], yd_ref[...])
    merged = None
    for n in range(N_BRANCHES):
        gate = _sigmoid(_dot(h, wg_ref[n]) + gb_ref[n])
        term = gate * _dot(ys[n], wb_ref[n])
        merged = term if merged is None else merged + term
    y = _dot(merged.astype(BF16), wo_ref[...])
    out_ref[...] = x + _rms(y, gpost_ref[...])


def _merge(x2, ya, dil, yc, yd, gpre, gpost, w_gate, gate_bias, w_branch, w_out):
    T = x2.shape[0]
    tm = TOKEN_TILE
    D = D_MODEL
    W = BRANCH_WIDTH
    row = lambda w: pl.BlockSpec((tm, w), lambda i: (i, 0))
    single = lambda shape: pl.BlockSpec(shape, lambda i: (0,) * len(shape), pipeline_mode=pl.Buffered(1))
    (o0, l0), (o1, l1), (o2, l2) = dil
    return pl.pallas_call(
        _merge_kernel,
        grid=(T // tm,),
        in_specs=[row(D)] + [row(W)] * 9 + [_const_spec((1, D)), _const_spec((1, D)),
                  single((N_BRANCHES, D, D)), _const_spec((N_BRANCHES, 1, D)), single((N_BRANCHES, W, D)),
                  single((D, D))],
        out_specs=row(D),
        out_shape=jax.ShapeDtypeStruct((T, D), F32),
        compiler_params=_params("parallel"),
        name="merge",
    )(x2, ya, o0, l0, o1, l1, o2, l2, yc, yd, gpre, gpost, w_gate, gate_bias, w_branch, w_out)


def _ffn_kernel(x_ref, gpre_ref, gpost_ref, wg_ref, wu_ref, wd_ref, out_ref):
    x = x_ref[...]
    h = _rms(x, gpre_ref[...]).astype(BF16)
    g = _dot(h, wg_ref[...])
    u = _dot(h, wu_ref[...])
    a = (g * _sigmoid(g) * u).astype(BF16)
    f = _dot(a, wd_ref[...])
    out_ref[...] = x + _rms(f, gpost_ref[...])


def _ffn(x2, gpre, gpost, w_gate, w_up, w_down):
    T = x2.shape[0]
    tm = TOKEN_TILE
    D = D_MODEL
    row = pl.BlockSpec((tm, D), lambda i: (i, 0))
    single = lambda shape: pl.BlockSpec(shape, lambda i: (0,) * len(shape), pipeline_mode=pl.Buffered(1))
    return pl.pallas_call(
        _ffn_kernel,
        grid=(T // tm,),
        in_specs=[row, _const_spec((1, D)), _const_spec((1, D)), single((D, D_FF)), single((D, D_FF)),
                  single((D_FF, D))],
        out_specs=row,
        out_shape=jax.ShapeDtypeStruct((T, D), F32),
        compiler_params=_params("parallel"),
        name="swiglu",
    )(x2, gpre, gpost, w_gate, w_up, w_down)


def _layer(x2, B, S, l, prm):
    D = D_MODEL
    W = BRANCH_WIDTH
    row = lambda a: a.reshape(1, -1)
    w_in = prm["w_in"][l]
    w_mix = w_in[:, :MIX_COLS].astype(BF16)
    w_gate = w_in[:, MIX_COLS:].reshape(D, N_BRANCHES, D).transpose(1, 0, 2).astype(BF16)

    p_rwkv, p_dil, p_diff, p_conv = _inproj(x2, row(prm["norm_mix_pre"][l]), w_mix)

    zeros = jnp.zeros((64, W), F32)
    wa_up = jnp.concatenate([jnp.concatenate([prm["rwkv_w_up"][l], zeros], axis=1),
                             jnp.concatenate([zeros, prm["rwkv_a_up"][l]], axis=1)], axis=0).astype(BF16)
    ya = _rwkv(p_rwkv.reshape(B, S, RWKV_COLS), row(prm["rwkv_mu"][l]), row(prm["rwkv_w0"][l]),
               row(prm["rwkv_a0"][l]), wa_up, prm["rwkv_g_up"][l].astype(BF16), row(prm["rwkv_k_k"][l]),
               row(prm["rwkv_k_a"][l]), row(prm["rwkv_r_k"][l]), row(prm["rwkv_ln_g"][l]),
               row(prm["rwkv_ln_b"][l]))

    p_dil3 = p_dil.reshape(B, S, DIL_COLS)
    dil = [tuple(a.reshape(B * S, W) for a in _dilated_group(p_dil3, gi)) for gi in range(DIL_GROUPS)]

    lambda_init = 0.8 - 0.6 * math.exp(-0.3 * l)
    lamv = jnp.stack([prm["diff_lam_q1"][l], prm["diff_lam_k1"][l], prm["diff_lam_q2"][l], prm["diff_lam_k2"][l]])
    yc = _diff_attention(p_diff.reshape(B, S, DIFF_COLS), lamv, prm["diff_subln_g"][l], lambda_init)

    yd = _conformer_conv(p_conv.reshape(B, S, CONV_COLS), prm["conv_dw_w"][l].reshape(CONV_WIDTH, CONV_CHANNELS),
                         row(prm["conv_dw_b"][l]), row(prm["conv_ln_g"][l]), row(prm["conv_ln_b"][l]))

    x2 = _merge(x2, ya.reshape(B * S, W), dil, yc.reshape(B * S, W), yd.reshape(B * S, W),
                row(prm["norm_mix_pre"][l]), row(prm["norm_mix_post"][l]), w_gate,
                prm["gate_bias"][l].reshape(N_BRANCHES, 1, D), prm["w_branch"][l].astype(BF16),
                prm["w_out"][l].astype(BF16))

    return _ffn(x2, row(prm["norm_ffn_pre"][l]), row(prm["norm_ffn_post"][l]), prm["ffn_w_gate"][l].astype(BF16),
                prm["ffn_w_up"][l].astype(BF16), prm["ffn_w_down"][l].astype(BF16))


def kernel(x, norm_mix_pre, norm_mix_post, norm_ffn_pre, norm_ffn_post, w_in, gate_bias, rwkv_mu, rwkv_w0, rwkv_w_up, rwkv_a0, rwkv_a_up, rwkv_g_up, rwkv_k_k, rwkv_k_a, rwkv_r_k, rwkv_ln_g, rwkv_ln_b, diff_lam_q1, diff_lam_k1, diff_lam_q2, diff_lam_k2, diff_subln_g, conv_dw_w, conv_dw_b, conv_ln_g, conv_ln_b, w_branch, w_out, ffn_w_gate, ffn_w_up, ffn_w_down):
    prm = dict(locals())
    B, S, D = x.shape
    x2 = x.reshape(B * S, D)
    for l in range(w_in.shape[0]):
        x2 = _layer(x2, B, S, l, prm)
    return x2.reshape(B, S, D)
```

```python
import functools
import math

import numpy as np
import jax
import jax.numpy as jnp
from jax import lax
from jax.experimental import pallas as pl
from jax.experimental.pallas import tpu as pltpu

F32 = jnp.float32
BF16 = jnp.bfloat16

D_MODEL = 1024
DEPTH = 4
HEAD_DIM = 64
BRANCH_WIDTH = 256
N_BRANCHES = 4

RWKV_HEADS = 4
RWKV_WIDTH = 256
RWKV_COLS = 1024
RWKV_DECAY_SCALE = 0.606531
RWKV_LN_EPS = 64e-5
RWKV_CHUNK = 64
RWKV_GROUP = 4

DIL_PATTERNS = ((128, 1), (512, 4), (2048, 16))
DIL_GROUPS = 3
DIL_HEADS = 12
DIL_QKV = 768
DIL_COLS = 2304
DIL_BLOCK = 128
DIL_TILE = 2048
DIL_SUB = 512

DIFF_HEADS = 4
DIFF_QK_DIM = 32
DIFF_COLS = 768
DIFF_TILE = 256
DIFF_SUBLN_EPS = 1e-5

CONV_CHANNELS = 256
CONV_WIDTH = 31
CONV_COLS = 512
CONV_LN_EPS = 1e-5
CONV_HALO = 32

MIX_COLS = RWKV_COLS + DIL_COLS + DIFF_COLS + CONV_COLS
D_FF = 2816
NORM_EPS = 1e-6
NEG_INF = -1e30
_LOG2E = 1.4426950408889634

VMEM_LIMIT_BYTES = 56 * 1024 * 1024
TOKEN_TILE = 512
MXU_COLS = 256


def _dot(a, b):
    return jnp.dot(a, b, preferred_element_type=F32)


def _dot_nt(a, b):
    return lax.dot_general(a, b, (((1,), (1,)), ((), ())), preferred_element_type=F32)


def _dot_tn(a, b):
    return lax.dot_general(a, b, (((0,), (0,)), ((), ())), preferred_element_type=F32)


def _dot_split(x, m):
    hi = x.astype(BF16)
    lo = (x - hi.astype(F32)).astype(BF16)
    return _dot(hi, m) + _dot(lo, m)


def _dot_split_rhs(m, x):
    hi = x.astype(BF16)
    lo = (x - hi.astype(F32)).astype(BF16)
    return _dot(m, hi) + _dot(m, lo)


def _rms(x, gain):
    return x * lax.rsqrt(jnp.mean(x * x, axis=-1, keepdims=True) + NORM_EPS) * gain


def _sigmoid(x):
    return 1.0 / (1.0 + jnp.exp(-x))


def _const_spec(shape):
    nd = len(shape)
    return pl.BlockSpec(shape, lambda *_: (0,) * nd)


def _params(*sem):
    return pltpu.CompilerParams(dimension_semantics=sem, vmem_limit_bytes=VMEM_LIMIT_BYTES)


def _inproj_kernel(x_ref, g_ref, w_ref, o_rwkv, o_dil0, o_dil1, o_dil2, o_diff, o_conv, h_ref):
    tm = x_ref.shape[0]
    hf = _rms(x_ref[...], g_ref[...])
    h = hf.astype(BF16)
    slabs = D_MODEL // 128
    for j in range(slabs):
        h_ref[j] = hf[:, j * 128:(j + 1) * 128]

    def permuted(d):
        cols = [jnp.concatenate([h_ref[j, pl.ds(r, tm // d, stride=d), :] for r in range(d)], axis=0)
                for j in range(slabs)]
        return jnp.concatenate(cols, axis=1).astype(BF16)

    group_w = DIL_COLS // DIL_GROUPS
    col = 0
    for o_ref, width, lhs in ((o_rwkv, RWKV_COLS, h), (o_dil0, group_w, h),
                              (o_dil1, group_w, permuted(DIL_PATTERNS[1][1])),
                              (o_dil2, group_w, permuted(DIL_PATTERNS[2][1])),
                              (o_diff, DIFF_COLS, h), (o_conv, CONV_COLS, h)):
        for c0 in range(0, width, MXU_COLS):
            y = _dot(lhs, w_ref[:, col + c0:col + c0 + MXU_COLS])
            o_ref[:, c0:c0 + MXU_COLS] = y.astype(o_ref.dtype)
        col += width


def _inproj(x2, gain, w_mix):
    T = x2.shape[0]
    tm = TOKEN_TILE
    assert tm == DIL_SUB
    group_w = DIL_COLS // DIL_GROUPS
    row = lambda w: pl.BlockSpec((tm, w), lambda i: (i, 0))
    bf = lambda w: jax.ShapeDtypeStruct((T, w), BF16)
    return pl.pallas_call(
        _inproj_kernel,
        grid=(T // tm,),
        in_specs=[row(D_MODEL), _const_spec((1, D_MODEL)), _const_spec((D_MODEL, MIX_COLS))],
        out_specs=[row(RWKV_COLS), row(group_w), row(group_w), row(group_w), row(DIFF_COLS), row(CONV_COLS)],
        out_shape=[jax.ShapeDtypeStruct((T, RWKV_COLS), F32), bf(group_w), bf(group_w), bf(group_w),
                   bf(DIFF_COLS), jax.ShapeDtypeStruct((T, CONV_COLS), F32)],
        scratch_shapes=[pltpu.VMEM((D_MODEL // 128, tm, 128), F32)],
        compiler_params=_params("parallel"),
        name="inproj",
    )(x2, gain, w_mix)


_MASK_STRICT, _MASK_INCL, _MASK_EYE, _MASK_LEVEL0 = 0, 1, 2, 3
_INV_LEVELS = (1, 2, 4, 8, 16, 32)


def _rwkv_masks():
    n = RWKV_HEADS * RWKV_CHUNK
    r = np.arange(n)[:, None]
    c = np.arange(n)[None, :]
    same = (r // RWKV_CHUNK) == (c // RWKV_CHUNK)
    out = [same & (r > c), same & (r >= c), r == c]
    for s in _INV_LEVELS:
        out.append((r // (2 * s) == c // (2 * s)) & (r % (2 * s) >= s) & (c % (2 * s) < s))
    head = (r // RWKV_CHUNK) == (c // HEAD_DIM)
    out.append(head)
    return np.stack(out).astype(np.float32)


_MASK_HEAD = 3 + len(_INV_LEVELS)


def _chunk_sum_mats(ts):
    r = np.arange(ts)[:, None]
    c = np.arange(ts)[None, :]
    same = (r // RWKV_CHUNK) == (c // RWKV_CHUNK)
    return np.stack([same & (c <= r), same]).astype(np.float32)


def _head_sum_mat(width, group):
    r = np.arange(width)[:, None]
    c = np.arange(width)[None, :]
    return ((r // group) == (c // group)).astype(np.float32)


def _rwkv_kernel(p_ref, mu_ref, w0_ref, a0_ref, wa_ref, gup_ref, kk_ref, ka_ref, rk_ref, lng_ref, lnb_ref,
                 masks_ref, csum_ref, hsum_ref, o_ref,
                 state_ref, prev_ref, at_ref, bt_ref, kt_ref, rt_ref, bh_ref, kh_ref, v_ref, ec_ref, oc_ref,
                 lhs_ref, upd_ref, vr_ref, arb_ref, ut_ref, arkv_ref):
    ts = p_ref.shape[1]
    W = RWKV_WIDTH
    C = RWKV_CHUNK

    @pl.when(pl.program_id(1) == 0)
    def _():
        state_ref[...] = jnp.zeros_like(state_ref)
        prev_ref[...] = jnp.zeros_like(prev_ref)

    p = p_ref[0]
    row = lax.broadcasted_iota(jnp.int32, p.shape, 0)
    p_prev = jnp.where(row == 0, prev_ref[...], pltpu.roll(p, 1, 0))
    prev_ref[...] = p[ts - 1:ts, :]
    p = p + (p_prev - p) * mu_ref[...]

    r = p[:, 0:W]
    k = p[:, W:2 * W]
    v = p[:, 2 * W:3 * W]
    zwa = p[:, 3 * W:3 * W + 128]
    zg = p[:, 3 * W + 128:]
    lane = lax.broadcasted_iota(jnp.int32, zwa.shape, 1)
    tz = jnp.where(lane < 64, jnp.tanh(zwa), zwa)
    wa = _dot(tz.astype(BF16), wa_ref[...])
    logw = -RWKV_DECAY_SCALE * _sigmoid(w0_ref[...] + wa[:, :W])
    alpha = _sigmoid(a0_ref[...] + wa[:, W:])
    g = _dot(_sigmoid(zg).astype(BF16), gup_ref[...])

    hsum = hsum_ref[...]
    kk = k * kk_ref[...]
    kk = kk * lax.rsqrt(jnp.maximum(_dot_split(kk * kk, hsum), 1e-24))
    k2 = k * (1.0 + (alpha - 1.0) * ka_ref[...])
    bvec = kk * alpha

    cum = _dot_split_rhs(csum_ref[0], logw)
    cum_end = _dot_split_rhs(csum_ref[1], logw)
    e_neg = jnp.exp(-cum)
    e_rem = jnp.exp(cum_end - cum)
    at_ref[...] = -kk * jnp.exp(cum - logw)
    bt_ref[...] = bvec * e_neg
    kt_ref[...] = k2 * e_neg
    rt_ref[...] = r * jnp.exp(cum)
    bh_ref[...] = bvec * e_rem
    kh_ref[...] = k2 * e_rem
    v_ref[...] = v
    ec_ref[...] = jnp.exp(cum_end)

    def tile4(x):
        x4 = jnp.concatenate([x, x, x, x], axis=0)
        return jnp.where(masks_ref[_MASK_HEAD] > 0, x4, 0.0).astype(BF16)

    def keep(m, x):
        return jnp.where(masks_ref[m] > 0, x, 0.0)

    n = RWKV_HEADS * C

    group = RWKV_GROUP

    def prepare(gi, carry):
        cs = [gi * group + j for j in range(group)]
        rows = [pl.ds(pl.multiple_of(c * C, C), C) for c in cs]
        at = [tile4(at_ref[r, :]) for r in rows]
        vr = [tile4(v_ref[r, :]) for r in rows]
        a_ab, a_ak = [], []
        for j, (c, r) in enumerate(zip(cs, rows)):
            rt = tile4(rt_ref[r, :])
            lhs = jnp.concatenate([at[j], rt], axis=0)
            rhs = jnp.concatenate([tile4(bt_ref[r, :]), tile4(kt_ref[r, :])], axis=0)
            a = _dot_nt(lhs, rhs)
            a_ab.append(keep(_MASK_STRICT, a[:n, :n]))
            a_ak.append(keep(_MASK_STRICT, a[:n, n:]).astype(BF16))
            arb_ref[c] = keep(_MASK_INCL, a[n:, :n]).astype(BF16)
            arkv_ref[c] = _dot(keep(_MASK_INCL, a[n:, n:]).astype(BF16), vr[j])
            lhs_ref[c, n:2 * n, :] = rt
            vr_ref[c] = vr[j]
            upd_ref[c, 0:n, :] = tile4(bh_ref[r, :])
            upd_ref[c, n:2 * n, :] = tile4(kh_ref[r, :])
        akv = [_dot(a_ak[j], vr[j]).astype(BF16) for j in range(group)]
        t = [masks_ref[_MASK_EYE] + keep(_MASK_LEVEL0, a_ab[j]) for j in range(group)]
        for li in range(1, len(_INV_LEVELS)):
            tb = [t[j].astype(BF16) for j in range(group)]
            xt = [_dot(keep(_MASK_LEVEL0 + li, a_ab[j]).astype(BF16), tb[j]).astype(BF16) for j in range(group)]
            t = [t[j] + _dot(tb[j], xt[j]) for j in range(group)]
        for j, c in enumerate(cs):
            tb = t[j].astype(BF16)
            lhs_ref[c, 0:n, :] = _dot(tb, at[j]).astype(BF16)
            ut_ref[c] = _dot(tb, akv[j])
        return carry

    lax.fori_loop(0, ts // (C * group), prepare, 0)

    def advance(c, carry):
        r0 = pl.multiple_of(c * C, C)
        s = state_ref[...]
        g_us = _dot_nt(lhs_ref[c], s.astype(BF16))
        ub = (g_us[:n] + ut_ref[c]).astype(BF16)
        o = g_us[n:] + _dot(arb_ref[c], ub) + arkv_ref[c]
        oc_ref[pl.ds(r0, C), :] = o[0:C] + o[C:2 * C] + o[2 * C:3 * C] + o[3 * C:4 * C]
        upd = _dot_tn(jnp.concatenate([ub, vr_ref[c]], axis=0), upd_ref[c])
        state_ref[...] = s * ec_ref[pl.ds(r0, 1), :] + upd
        return carry

    lax.fori_loop(0, ts // C, advance, 0)

    o = oc_ref[...]
    hmean = hsum * (1.0 / HEAD_DIM)
    mean = _dot_split(o, hmean)
    d = o - mean
    var = _dot_split(d * d, hmean)
    o = d * lax.rsqrt(var + RWKV_LN_EPS) * lng_ref[...] + lnb_ref[...]
    bonus = _dot_split(r * k2 * rk_ref[...], hsum)
    o_ref[0] = ((o + bonus * v) * g).astype(o_ref.dtype)


def _rwkv(p, mu, w0, a0, wa_up, g_up, k_k, k_a, r_k, ln_g, ln_b):
    B, S, _ = p.shape
    ts = TOKEN_TILE
    W = RWKV_WIDTH
    n = RWKV_HEADS * RWKV_CHUNK
    nc = ts // RWKV_CHUNK
    masks = jnp.asarray(_rwkv_masks())
    csum = jnp.asarray(_chunk_sum_mats(ts), BF16)
    hsum = jnp.asarray(_head_sum_mat(W, HEAD_DIM), BF16)
    vec = _const_spec((1, W))
    buf = lambda: pltpu.VMEM((ts, W), F32)
    return pl.pallas_call(
        _rwkv_kernel,
        grid=(B, S // ts),
        in_specs=[pl.BlockSpec((1, ts, RWKV_COLS), lambda b, s: (b, s, 0)),
                  _const_spec((1, RWKV_COLS)), vec, vec, _const_spec((128, 2 * W)), _const_spec((128, W)),
                  vec, vec, vec, vec, vec,
                  _const_spec(masks.shape), _const_spec(csum.shape), _const_spec(hsum.shape)],
        out_specs=pl.BlockSpec((1, ts, W), lambda b, s: (b, s, 0)),
        out_shape=jax.ShapeDtypeStruct((B, S, W), BF16),
        scratch_shapes=[pltpu.VMEM((n, n), F32), pltpu.VMEM((1, RWKV_COLS), F32)] + [buf() for _ in range(9)]
                       + [pltpu.VMEM((nc, 2 * n, n), BF16), pltpu.VMEM((nc, 2 * n, n), BF16),
                          pltpu.VMEM((nc, n, n), BF16), pltpu.VMEM((nc, n, n), BF16),
                          pltpu.VMEM((nc, n, n), F32), pltpu.VMEM((nc, n, n), F32)],
        compiler_params=_params("parallel", "arbitrary"),
        name="rwkv7",
    )(p, mu, w0, a0, wa_up, g_up, k_k, k_a, r_k, ln_g, ln_b, masks, csum, hsum)


def _alibi_slopes(n):
    return 2.0 ** (-8.0 * np.arange(1, n + 1) / n)


def _dil_bias(group):
    window, dilation = DIL_PATTERNS[group]
    n = window // dilation
    slopes = _alibi_slopes(DIL_HEADS).reshape(DIL_GROUPS, -1)[group]
    rel = n + np.arange(n)[:, None] - np.arange(2 * n)[None, :]
    valid = (rel >= 0) & (rel <= n)
    bias = -slopes[:, None, None] * (dilation * rel).astype(np.float64)[None]
    return np.where(valid[None], bias, NEG_INF).astype(np.float32)


def _dil_attend(q, k, v, bias_of, front):
    heads = q.shape[1] // HEAD_DIM
    head = lax.broadcasted_iota(jnp.int32, q.shape, 1) // HEAD_DIM
    s = [_dot_nt(jnp.where(head == h, q, jnp.zeros_like(q)), k) + bias_of(h) for h in range(heads)]
    if front is not None:
        s = [jnp.where(front, NEG_INF, x) for x in s]
    m = [jnp.max(x, axis=-1, keepdims=True) for x in s]
    e = [jnp.exp2(x - mx) for x, mx in zip(s, m)]
    l = [jnp.sum(x, axis=-1, keepdims=True) for x in e]
    pv = [_dot(x.astype(BF16), v) for x in e]
    out = jnp.zeros(q.shape, F32)
    lse = jnp.zeros(q.shape, F32)
    for h in range(heads):
        out = jnp.where(head == h, pv[h] / l[h], out)
        lse = jnp.where(head == h, m[h] + jnp.log(l[h]) * _LOG2E, lse)
    return out, lse


def _dil_kernel(c0_ref, c1_ref, c2_ref, p0_ref, p1_ref, p2_ref, b0_ref, b1_ref, b2_ref, y_ref, og_ref, lg_ref):
    n = DIL_BLOCK
    W = BRANCH_WIDTH
    sub = DIL_SUB
    first = pl.program_id(1) == 0
    front = jnp.logical_and(first, lax.broadcasted_iota(jnp.int32, (n, 2 * n), 1) < n)
    qscale = HEAD_DIM ** -0.5 * _LOG2E
    kcols, vcols, qcols = slice(0, W), slice(W, 2 * W), slice(2 * W, 3 * W)

    def scaled(q):
        return (q.astype(F32) * qscale).astype(BF16)

    def put(g, rows, out, lse):
        for half in range(W // 128):
            lanes = slice(half * 128, (half + 1) * 128)
            og_ref[g, half, rows, :] = out[:, lanes]
            lg_ref[g, half, rows, :] = lse[:, lanes]

    prev = p0_ref[0]
    cur = c0_ref[0, 0:n, :]
    out, lse = _dil_attend(scaled(cur[:, qcols]), jnp.concatenate([prev[:, kcols], cur[:, kcols]], axis=0),
                           jnp.concatenate([prev[:, vcols], cur[:, vcols]], axis=0), lambda h: b0_ref[h], front)
    put(0, pl.ds(0, n), out, lse)

    def g0_body(c, carry):
        r0 = pl.multiple_of((c - 1) * n, n)
        kv = c0_ref[0, pl.ds(r0, 2 * n), 0:2 * W]
        q = c0_ref[0, pl.ds(r0 + n, n), qcols]
        out, lse = _dil_attend(scaled(q), kv[:, kcols], kv[:, vcols], lambda h: b0_ref[h], None)
        put(0, pl.ds(r0 + n, n), out, lse)
        return carry

    lax.fori_loop(1, DIL_TILE // n, g0_body, 0)

    d1 = DIL_PATTERNS[1][1]

    def g1_block(u, r, prev, front_mask):
        cur = c1_ref[0, pl.ds(pl.multiple_of(u * sub + r * n, n), n), :]
        out, lse = _dil_attend(scaled(cur[:, qcols]), jnp.concatenate([prev[:, kcols], cur[:, kcols]], axis=0),
                               jnp.concatenate([prev[:, vcols], cur[:, vcols]], axis=0), lambda h: b1_ref[h],
                               front_mask)
        put(1, pl.ds(u * sub + r, n, stride=d1), out, lse)

    def g1_first(r, carry):
        g1_block(0, r, p1_ref[0, pl.ds(pl.multiple_of(r * n, n), n), :], front)
        return carry

    lax.fori_loop(0, d1, g1_first, 0)

    def g1_rest(idx, carry):
        u = 1 + idx // d1
        r = idx % d1
        prev = c1_ref[0, pl.ds(pl.multiple_of((u - 1) * sub + r * n, n), n), 0:2 * W]
        g1_block(u, r, prev, None)
        return carry

    lax.fori_loop(0, (DIL_TILE // sub - 1) * d1, g1_rest, 0)

    d2 = DIL_PATTERNS[2][1]
    piece = sub // d2

    def g2_body(r, carry):
        def rows_of(ref, cols):
            return jnp.concatenate(
                [ref[0, pl.ds(pl.multiple_of(u * sub + r * piece, piece), piece), cols]
                 for u in range(DIL_TILE // sub)], axis=0)
        cur = rows_of(c2_ref, slice(0, 3 * W))
        prev = rows_of(p2_ref, slice(0, 2 * W))
        out, lse = _dil_attend(scaled(cur[:, qcols]), jnp.concatenate([prev[:, kcols], cur[:, kcols]], axis=0),
                               jnp.concatenate([prev[:, vcols], cur[:, vcols]], axis=0), lambda h: b2_ref[h], front)
        put(2, pl.ds(r, n, stride=d2), out, lse)
        return carry

    lax.fori_loop(0, d2, g2_body, 0)

    rows_per = 256

    def combine(j, carry):
        rows = pl.ds(pl.multiple_of(j * rows_per, rows_per), rows_per)
        for half in range(W // 128):
            ls = [lg_ref[g, half, rows, :] for g in range(DIL_GROUPS)]
            m = jnp.maximum(jnp.maximum(ls[0], ls[1]), ls[2])
            es = [jnp.exp2(x - m) for x in ls]
            num = es[0] * og_ref[0, half, rows, :] + es[1] * og_ref[1, half, rows, :] + es[2] * og_ref[2, half, rows, :]
            y_ref[0, rows, half * 128:(half + 1) * 128] = (num / (es[0] + es[1] + es[2])).astype(y_ref.dtype)
        return carry

    lax.fori_loop(0, DIL_TILE // rows_per, combine, 0)


def _dilated(pd0, pd1, pd2):
    B, S, _ = pd0.shape
    W = BRANCH_WIDTH
    T = DIL_TILE
    assert S % T == 0
    for (window, d) in DIL_PATTERNS:
        assert window // d == DIL_BLOCK
    assert DIL_PATTERNS[0][1] == 1 and DIL_PATTERNS[1][1] * DIL_BLOCK == DIL_SUB and DIL_PATTERNS[2][1] * DIL_BLOCK == T
    biases = [jnp.asarray(_dil_bias(g) * _LOG2E) for g in range(DIL_GROUPS)]
    cur = pl.BlockSpec((1, T, 3 * W), lambda b, i: (b, i, 0))
    prev = lambda rows: pl.BlockSpec((1, rows, 2 * W), lambda b, i: (b, jnp.maximum(i * (T // rows) - 1, 0), 0))
    return pl.pallas_call(
        _dil_kernel,
        grid=(B, S // T),
        in_specs=[cur, cur, cur, prev(DIL_BLOCK), prev(DIL_SUB), prev(T)] + [_const_spec(b.shape) for b in biases],
        out_specs=pl.BlockSpec((1, T, W), lambda b, i: (b, i, 0)),
        out_shape=jax.ShapeDtypeStruct((B, S, W), BF16),
        scratch_shapes=[pltpu.VMEM((DIL_GROUPS, W // 128, T, 128), F32)] * 2,
        compiler_params=_params("parallel", "arbitrary"),
        name="dilated",
    )(pd0, pd1, pd2, pd0, pd1, pd2, *biases)


def _diff_bias():
    t = DIFF_TILE
    slopes = _alibi_slopes(DIFF_HEADS) * _LOG2E
    rel = (np.arange(t)[:, None] - np.arange(2 * t)[None, :]).astype(np.float64)
    off = -slopes[:, None, None] * rel[None]
    diag = np.where(rel[None] >= 0, off, NEG_INF)[:, :, :t]
    return off.astype(np.float32), diag.astype(np.float32)


def _diff_kernel(q_ref, k_ref, v_ref, boff_ref, bdiag_ref, lamv_ref, gain_ref, hsum_ref, o_ref, m_ref, acc_ref,
                 *, lambda_init):
    t = DIFF_TILE
    half = 2 * HEAD_DIM
    n_maps = 2 * DIFF_HEADS
    qi = pl.program_id(1)
    q = (q_ref[0].astype(F32) * (DIFF_QK_DIM ** -0.5 * _LOG2E)).astype(BF16)
    lane = lax.broadcasted_iota(jnp.int32, q.shape, 1)
    qs = [jnp.where(lane // DIFF_QK_DIM == c, q, jnp.zeros_like(q)) for c in range(n_maps)]
    slopes = _alibi_slopes(DIFF_HEADS) * _LOG2E
    low = lax.broadcasted_iota(jnp.int32, (t, half), 1) < HEAD_DIM

    m_ref[...] = jnp.full(m_ref.shape, NEG_INF, F32)
    acc_ref[...] = jnp.zeros_like(acc_ref)

    def block(k0, nk, bias_of):
        rows = pl.ds(k0, nk)
        k = k_ref[0, rows, :]
        v = v_ref[0, rows, :]
        low_k = lax.broadcasted_iota(jnp.int32, (nk, half), 1) < HEAD_DIM
        ones = jnp.ones((nk, half), BF16)
        vh = []
        for h in range(DIFF_HEADS):
            vv = v[:, half * (h // 2):half * (h // 2 + 1)]
            vh.append(jnp.where(low_k, vv, ones) if h % 2 == 0 else jnp.where(low_k, ones, vv))
        shift = (qi * t - k0).astype(F32)
        block_bias = [float(slopes[c // 2]) * shift for c in range(n_maps)]
        s = [_dot_nt(qs[c], k) + bias_of(c // 2) for c in range(n_maps)]
        m_prev = [m_ref[c] for c in range(n_maps)]
        m_new = [jnp.maximum(m_prev[c], jnp.max(s[c], axis=-1, keepdims=True) - block_bias[c])
                 for c in range(n_maps)]
        e = []
        for c in range(n_maps):
            ms = m_new[c] + block_bias[c]
            e.append(jnp.exp2(s[c] - jnp.concatenate([ms] * (nk // half), axis=1)).astype(BF16))
        for c in range(n_maps):
            m_ref[c] = m_new[c]
            acc_ref[c] = acc_ref[c] * jnp.exp2(m_prev[c] - m_new[c]) + _dot(e[c], vh[c // 2])

    def body(j, carry):
        block(pl.multiple_of(j * (2 * t), 2 * t), 2 * t, lambda h: boff_ref[h])
        return carry

    lax.fori_loop(0, qi // 2, body, 0)

    @pl.when(qi % 2 == 1)
    def _():
        block(pl.multiple_of((qi - 1) * t, t), t, lambda h: boff_ref[h, :, 0:t])

    block(pl.multiple_of(qi * t, t), t, lambda h: bdiag_ref[h])

    lamv = lamv_ref[...]
    lam = (jnp.exp(jnp.sum(lamv[0:1] * lamv[1:2], axis=-1, keepdims=True))
           - jnp.exp(jnp.sum(lamv[2:3] * lamv[3:4], axis=-1, keepdims=True)) + lambda_init)
    heads = []
    for h in range(DIFF_HEADS):
        a0 = acc_ref[2 * h]
        a1 = acc_ref[2 * h + 1]
        heads.append(a0 / pltpu.roll(a0, HEAD_DIM, 1) - lam * (a1 / pltpu.roll(a1, HEAD_DIM, 1)))
    o = jnp.concatenate([jnp.where(low, heads[0], heads[1]), jnp.where(low, heads[2], heads[3])], axis=1)
    ms = _dot_split(o * o, hsum_ref[...]) * (1.0 / HEAD_DIM)
    o = o * lax.rsqrt(ms + DIFF_SUBLN_EPS) * gain_ref[...]
    o_ref[0] = (o * (1.0 - lambda_init)).astype(o_ref.dtype)


def _diff_attention(p_diff, lamv, subln_g, lambda_init):
    B, S, _ = p_diff.shape
    t = DIFF_TILE
    W = BRANCH_WIDTH
    boff, bdiag = (jnp.asarray(b) for b in _diff_bias())
    hsum = jnp.asarray(_head_sum_mat(W, HEAD_DIM), BF16)
    gain = jnp.tile(subln_g.reshape(1, HEAD_DIM), (1, DIFF_HEADS))
    full = lambda col: pl.BlockSpec((1, S, W), lambda b, i: (b, 0, col))
    return pl.pallas_call(
        functools.partial(_diff_kernel, lambda_init=lambda_init),
        grid=(B, S // t),
        in_specs=[pl.BlockSpec((1, t, W), lambda b, i: (b, i, 0)), full(1), full(2),
                  _const_spec(boff.shape), _const_spec(bdiag.shape), _const_spec(lamv.shape), _const_spec((1, W)),
                  _const_spec((W, W))],
        out_specs=pl.BlockSpec((1, t, W), lambda b, i: (b, i, 0)),
        out_shape=jax.ShapeDtypeStruct((B, S, W), BF16),
        scratch_shapes=[pltpu.VMEM((2 * DIFF_HEADS, t, 128), F32), pltpu.VMEM((2 * DIFF_HEADS, t, 128), F32)],
        compiler_params=_params("parallel", "arbitrary"),
        name="diffattn",
    )(p_diff, p_diff, p_diff, boff, bdiag, lamv, gain, hsum)


def _conv_kernel(p_ref, w_ref, b_ref, g_ref, beta_ref, o_ref, u_ref):
    ts = p_ref.shape[1]
    Cc = CONV_CHANNELS
    halo = CONV_HALO
    rows = 64

    @pl.when(pl.program_id(1) == 0)
    def _():
        u_ref[0:halo, :] = jnp.zeros((halo, Cc), F32)

    @pl.when(pl.program_id(1) > 0)
    def _():
        u_ref[0:halo, :] = u_ref[ts:ts + halo, :]

    p = p_ref[0]
    u_ref[halo:halo + ts, :] = p[:, :Cc] * _sigmoid(p[:, Cc:])

    def block(i, carry):
        r0 = pl.multiple_of(i * rows, rows)
        acc = jnp.zeros((rows, Cc), F32) + b_ref[...]
        win = u_ref[pl.ds(r0, rows + halo), :]
        first_tap = halo - CONV_WIDTH + 1
        for b in range(8):
            wb = win if b == 0 else pltpu.roll(win, rows + halo - b, 0)
            for a in range(halo // 8 + 1):
                j = 8 * a + b - first_tap
                if 0 <= j < CONV_WIDTH:
                    acc = acc + w_ref[j:j + 1, :] * wb[8 * a:8 * a + rows]
        mu = jnp.mean(acc, axis=-1, keepdims=True)
        d = acc - mu
        var = jnp.mean(d * d, axis=-1, keepdims=True)
        y = d * lax.rsqrt(var + CONV_LN_EPS) * g_ref[...] + beta_ref[...]
        o_ref[0, pl.ds(r0, rows), :] = (y * _sigmoid(y)).astype(o_ref.dtype)
        return carry

    lax.fori_loop(0, ts // rows, block, 0)


def _conformer_conv(p_conv, dw_w, dw_b, ln_g, ln_b):
    B, S, _ = p_conv.shape
    ts = TOKEN_TILE
    Cc = CONV_CHANNELS
    vec = _const_spec((1, Cc))
    return pl.pallas_call(
        _conv_kernel,
        grid=(B, S // ts),
        in_specs=[pl.BlockSpec((1, ts, CONV_COLS), lambda b, s: (b, s, 0)), _const_spec((CONV_WIDTH, Cc)),
                  vec, vec, vec],
        out_specs=pl.BlockSpec((1, ts, Cc), lambda b, s: (b, s, 0)),
        out_shape=jax.ShapeDtypeStruct((B, S, Cc), BF16),
        scratch_shapes=[pltpu.VMEM((ts + CONV_HALO, Cc), F32)],
        compiler_params=_params("parallel", "arbitrary"),
        name="conformer_conv",
    )(p_conv, dw_w, dw_b, ln_g, ln_b)


def _merge_kernel(x_ref, ya_ref, yb_ref, yc_ref, yd_ref,
                  gpre_ref, gpost_ref, wg_ref, gb_ref, wb_ref, wo_ref, out_ref):
    x = x_ref[...]
    h = _rms(x, gpre_ref[...]).astype(BF16)
    ys = (ya_ref[...], yb_ref[...], yc_ref[...], yd_ref[...])
    merged = None
    for n in range(N_BRANCHES):
        gate = _sigmoid(_dot(h, wg_ref[n]) + gb_ref[n])
        term = gate * _dot(ys[n], wb_ref[n])
        merged = term if merged is None else merged + term
    y = _dot(merged.astype(BF16), wo_ref[...])
    out_ref[...] = x + _rms(y, gpost_ref[...])


def _merge(x2, ya, yb, yc, yd, gpre, gpost, w_gate, gate_bias, w_branch, w_out):
    T = x2.shape[0]
    tm = TOKEN_TILE
    D = D_MODEL
    W = BRANCH_WIDTH
    row = lambda w: pl.BlockSpec((tm, w), lambda i: (i, 0))
    single = lambda shape: pl.BlockSpec(shape, lambda i: (0,) * len(shape), pipeline_mode=pl.Buffered(1))
    return pl.pallas_call(
        _merge_kernel,
        grid=(T // tm,),
        in_specs=[row(D)] + [row(W)] * 4 + [_const_spec((1, D)), _const_spec((1, D)),
                  single((N_BRANCHES, D, D)), _const_spec((N_BRANCHES, 1, D)), single((N_BRANCHES, W, D)),
                  single((D, D))],
        out_specs=row(D),
        out_shape=jax.ShapeDtypeStruct((T, D), F32),
        compiler_params=_params("parallel"),
        name="merge",
    )(x2, ya, yb, yc, yd, gpre, gpost, w_gate, gate_bias, w_branch, w_out)


def _ffn_kernel(x_ref, gpre_ref, gpost_ref, wg_ref, wu_ref, wd_ref, out_ref):
    x = x_ref[...]
    h = _rms(x, gpre_ref[...]).astype(BF16)
    g = _dot(h, wg_ref[...])
    u = _dot(h, wu_ref[...])
    a = (g * _sigmoid(g) * u).astype(BF16)
    f = _dot(a, wd_ref[...])
    out_ref[...] = x + _rms(f, gpost_ref[...])


def _ffn(x2, gpre, gpost, w_gate, w_up, w_down):
    T = x2.shape[0]
    tm = TOKEN_TILE
    D = D_MODEL
    row = pl.BlockSpec((tm, D), lambda i: (i, 0))
    single = lambda shape: pl.BlockSpec(shape, lambda i: (0,) * len(shape), pipeline_mode=pl.Buffered(1))
    return pl.pallas_call(
        _ffn_kernel,
        grid=(T // tm,),
        in_specs=[row, _const_spec((1, D)), _const_spec((1, D)), single((D, D_FF)), single((D, D_FF)),
                  single((D_FF, D))],
        out_specs=row,
        out_shape=jax.ShapeDtypeStruct((T, D), F32),
        compiler_params=_params("parallel"),
        name="swiglu",
    )(x2, gpre, gpost, w_gate, w_up, w_down)


def _layer(x2, B, S, l, prm):
    D = D_MODEL
    W = BRANCH_WIDTH
    row = lambda a: a.reshape(1, -1)
    w_in = prm["w_in"][l]
    w_dil = w_in[:, RWKV_COLS:RWKV_COLS + DIL_COLS].reshape(D, 3, DIL_GROUPS, W)
    w_dil = jnp.concatenate([w_dil[:, 1], w_dil[:, 2], w_dil[:, 0]], axis=-1).transpose(1, 0, 2)
    w_mix = jnp.concatenate([w_in[:, :RWKV_COLS]] + [w_dil[g] for g in range(DIL_GROUPS)]
                            + [w_in[:, RWKV_COLS + DIL_COLS:MIX_COLS]], axis=1).astype(BF16)
    w_gate = w_in[:, MIX_COLS:].reshape(D, N_BRANCHES, D).transpose(1, 0, 2).astype(BF16)

    p_rwkv, pd0, pd1, pd2, p_diff, p_conv = _inproj(x2, row(prm["norm_mix_pre"][l]), w_mix)

    zeros = jnp.zeros((64, W), F32)
    wa_up = jnp.concatenate([jnp.concatenate([prm["rwkv_w_up"][l], zeros], axis=1),
                             jnp.concatenate([zeros, prm["rwkv_a_up"][l]], axis=1)], axis=0).astype(BF16)
    ya = _rwkv(p_rwkv.reshape(B, S, RWKV_COLS), row(prm["rwkv_mu"][l]), row(prm["rwkv_w0"][l]),
               row(prm["rwkv_a0"][l]), wa_up, prm["rwkv_g_up"][l].astype(BF16), row(prm["rwkv_k_k"][l]),
               row(prm["rwkv_k_a"][l]), row(prm["rwkv_r_k"][l]), row(prm["rwkv_ln_g"][l]),
               row(prm["rwkv_ln_b"][l]))

    group_w = DIL_COLS // DIL_GROUPS
    yb = _dilated(pd0.reshape(B, S, group_w), pd1.reshape(B, S, group_w), pd2.reshape(B, S, group_w))

    lambda_init = 0.8 - 0.6 * math.exp(-0.3 * l)
    lamv = jnp.stack([prm["diff_lam_q1"][l], prm["diff_lam_k1"][l], prm["diff_lam_q2"][l], prm["diff_lam_k2"][l]])
    yc = _diff_attention(p_diff.reshape(B, S, DIFF_COLS), lamv, prm["diff_subln_g"][l], lambda_init)

    yd = _conformer_conv(p_conv.reshape(B, S, CONV_COLS), prm["conv_dw_w"][l].reshape(CONV_WIDTH, CONV_CHANNELS),
                         row(prm["conv_dw_b"][l]), row(prm["conv_ln_g"][l]), row(prm["conv_ln_b"][l]))

    x2 = _merge(x2, ya.reshape(B * S, W), yb.reshape(B * S, W), yc.reshape(B * S, W), yd.reshape(B * S, W),
                row(prm["norm_mix_pre"][l]), row(prm["norm_mix_post"][l]), w_gate,
                prm["gate_bias"][l].reshape(N_BRANCHES, 1, D), prm["w_branch"][l].astype(BF16),
                prm["w_out"][l].astype(BF16))

    return _ffn(x2, row(prm["norm_ffn_pre"][l]), row(prm["norm_ffn_post"][l]), prm["ffn_w_gate"][l].astype(BF16),
                prm["ffn_w_up"][l].astype(BF16), prm["ffn_w_down"][l].astype(BF16))


def kernel(x, norm_mix_pre, norm_mix_post, norm_ffn_pre, norm_ffn_post, w_in, gate_bias, rwkv_mu, rwkv_w0, rwkv_w_up, rwkv_a0, rwkv_a_up, rwkv_g_up, rwkv_k_k, rwkv_k_a, rwkv_r_k, rwkv_ln_g, rwkv_ln_b, diff_lam_q1, diff_lam_k1, diff_lam_q2, diff_lam_k2, diff_subln_g, conv_dw_w, conv_dw_b, conv_ln_g, conv_ln_b, w_branch, w_out, ffn_w_gate, ffn_w_up, ffn_w_down):
    prm = dict(locals())
    B, S, D = x.shape
    x2 = x.reshape(B * S, D)
    for l in range(w_in.shape[0]):
        x2 = _layer(x2, B, S, l, prm)
    return x2.reshape(B, S, D)
```

```python
import functools
import math

import numpy as np
import jax
import jax.numpy as jnp
from jax import lax
from jax.experimental import pallas as pl
from jax.experimental.pallas import tpu as pltpu

F32 = jnp.float32
BF16 = jnp.bfloat16

D_MODEL = 1024
DEPTH = 4
HEAD_DIM = 64
BRANCH_WIDTH = 256
N_BRANCHES = 4

RWKV_HEADS = 4
RWKV_WIDTH = 256
RWKV_COLS = 1024
RWKV_DECAY_SCALE = 0.606531
RWKV_LN_EPS = 64e-5
RWKV_CHUNK = 64
RWKV_GROUP = 4

DIL_PATTERNS = ((128, 1), (512, 4), (2048, 16))
DIL_GROUPS = 3
DIL_HEADS = 12
DIL_QKV = 768
DIL_COLS = 2304
DIL_BLOCK = 128
DIL_TILE = 2048
DIL_SUB = 512
DIL_G0_BLOCKS_PER_TRIP = 3
DIL_G2_BLOCKS_PER_TRIP = 4

DIFF_HEADS = 4
DIFF_QK_DIM = 32
DIFF_COLS = 768
DIFF_TILE = 512
DIFF_QK_COLS = 512
DIFF_ACC_ROWS = HEAD_DIM + 16
DIFF_SUBLN_EPS = 1e-5

CONV_CHANNELS = 256
CONV_WIDTH = 31
CONV_COLS = 512
CONV_LN_EPS = 1e-5
CONV_HALO = 32

MIX_COLS = RWKV_COLS + DIL_COLS + DIFF_COLS + CONV_COLS
D_FF = 2816
NORM_EPS = 1e-6
NEG_INF = -1e30
_LOG2E = 1.4426950408889634

VMEM_LIMIT_BYTES = 56 * 1024 * 1024
TOKEN_TILE = 512
MXU_COLS = 256


def _dot(a, b):
    return jnp.dot(a, b, preferred_element_type=F32)


def _dot_nt(a, b):
    return lax.dot_general(a, b, (((1,), (1,)), ((), ())), preferred_element_type=F32)


def _dot_tn(a, b):
    return lax.dot_general(a, b, (((0,), (0,)), ((), ())), preferred_element_type=F32)


def _dot_split(x, m):
    hi = x.astype(BF16)
    lo = (x - hi.astype(F32)).astype(BF16)
    return _dot(hi, m) + _dot(lo, m)


def _dot_split_rhs(m, x):
    hi = x.astype(BF16)
    lo = (x - hi.astype(F32)).astype(BF16)
    return _dot(m, hi) + _dot(m, lo)


def _rms(x, gain):
    return x * lax.rsqrt(jnp.mean(x * x, axis=-1, keepdims=True) + NORM_EPS) * gain


def _sigmoid(x):
    return 1.0 / (1.0 + jnp.exp(-x))


def _const_spec(shape):
    nd = len(shape)
    return pl.BlockSpec(shape, lambda *_: (0,) * nd)


def _params(*sem):
    return pltpu.CompilerParams(dimension_semantics=sem, vmem_limit_bytes=VMEM_LIMIT_BYTES)


def _inproj_kernel(x_ref, g_ref, w_ref, wvt_ref, o_rwkv, o_dil0, o_dil1, o_dil2, o_diff, o_conv, o_vt, h_ref):
    tm = x_ref.shape[0]
    hf = _rms(x_ref[...], g_ref[...])
    h = hf.astype(BF16)
    slabs = D_MODEL // 128
    for j in range(slabs):
        h_ref[j] = hf[:, j * 128:(j + 1) * 128]

    def permuted(d):
        cols = [jnp.concatenate([h_ref[j, pl.ds(r, tm // d, stride=d), :] for r in range(d)], axis=0)
                for j in range(slabs)]
        return jnp.concatenate(cols, axis=1).astype(BF16)

    group_w = DIL_COLS // DIL_GROUPS
    col = 0
    for o_ref, width, lhs in ((o_rwkv, RWKV_COLS, h), (o_dil0, group_w, h),
                              (o_dil1, group_w, permuted(DIL_PATTERNS[1][1])),
                              (o_dil2, group_w, permuted(DIL_PATTERNS[2][1])),
                              (o_diff, DIFF_QK_COLS, h), (o_conv, CONV_COLS, h)):
        for c0 in range(0, width, MXU_COLS):
            y = _dot(lhs, w_ref[:, col + c0:col + c0 + MXU_COLS])
            o_ref[:, c0:c0 + MXU_COLS] = y.astype(o_ref.dtype)
        col += width
    o_vt[...] = _dot_nt(wvt_ref[...], h).astype(o_vt.dtype)


def _inproj(x2, gain, w_mix, w_vt):
    T = x2.shape[0]
    tm = TOKEN_TILE
    assert tm == DIL_SUB
    group_w = DIL_COLS // DIL_GROUPS
    W = BRANCH_WIDTH
    row = lambda w: pl.BlockSpec((tm, w), lambda i: (i, 0))
    bf = lambda w: jax.ShapeDtypeStruct((T, w), BF16)
    return pl.pallas_call(
        _inproj_kernel,
        grid=(T // tm,),
        in_specs=[row(D_MODEL), _const_spec((1, D_MODEL)), _const_spec((D_MODEL, MIX_COLS - W)),
                  _const_spec((W, D_MODEL))],
        out_specs=[row(RWKV_COLS), row(group_w), row(group_w), row(group_w), row(DIFF_QK_COLS), row(CONV_COLS),
                   pl.BlockSpec((W, tm), lambda i: (0, i))],
        out_shape=[jax.ShapeDtypeStruct((T, RWKV_COLS), F32), bf(group_w), bf(group_w), bf(group_w),
                   bf(DIFF_QK_COLS), jax.ShapeDtypeStruct((T, CONV_COLS), F32),
                   jax.ShapeDtypeStruct((W, T), BF16)],
        scratch_shapes=[pltpu.VMEM((D_MODEL // 128, tm, 128), F32)],
        compiler_params=_params("parallel"),
        name="inproj",
    )(x2, gain, w_mix, w_vt)


_MASK_STRICT, _MASK_INCL, _MASK_EYE, _MASK_LEVEL0 = 0, 1, 2, 3
_INV_LEVELS = (1, 2, 4, 8, 16, 32)


def _rwkv_masks():
    n = RWKV_HEADS * RWKV_CHUNK
    r = np.arange(n)[:, None]
    c = np.arange(n)[None, :]
    same = (r // RWKV_CHUNK) == (c // RWKV_CHUNK)
    out = [same & (r > c), same & (r >= c), r == c]
    for s in _INV_LEVELS:
        out.append((r // (2 * s) == c // (2 * s)) & (r % (2 * s) >= s) & (c % (2 * s) < s))
    head = (r // RWKV_CHUNK) == (c // HEAD_DIM)
    out.append(head)
    return np.stack(out).astype(np.float32)


_MASK_HEAD = 3 + len(_INV_LEVELS)


def _chunk_sum_mats(ts):
    r = np.arange(ts)[:, None]
    c = np.arange(ts)[None, :]
    same = (r // RWKV_CHUNK) == (c // RWKV_CHUNK)
    return np.stack([same & (c <= r), same]).astype(np.float32)


def _head_sum_mat(width, group):
    r = np.arange(width)[:, None]
    c = np.arange(width)[None, :]
    return ((r // group) == (c // group)).astype(np.float32)


def _rwkv_kernel(p_ref, mu_ref, w0_ref, a0_ref, wa_ref, gup_ref, kk_ref, ka_ref, rk_ref, lng_ref, lnb_ref,
                 masks_ref, csum_ref, hsum_ref, o_ref,
                 state_ref, prev_ref, at_ref, bt_ref, kt_ref, rt_ref, bh_ref, kh_ref, v_ref, ec_ref, oc_ref,
                 lhs_ref, upd_ref, vr_ref, arb_ref, ut_ref, arkv_ref):
    ts = p_ref.shape[1]
    W = RWKV_WIDTH
    C = RWKV_CHUNK

    @pl.when(pl.program_id(1) == 0)
    def _():
        state_ref[...] = jnp.zeros_like(state_ref)
        prev_ref[...] = jnp.zeros_like(prev_ref)

    p = p_ref[0]
    row = lax.broadcasted_iota(jnp.int32, p.shape, 0)
    p_prev = jnp.where(row == 0, prev_ref[...], pltpu.roll(p, 1, 0))
    prev_ref[...] = p[ts - 1:ts, :]
    p = p + (p_prev - p) * mu_ref[...]

    r = p[:, 0:W]
    k = p[:, W:2 * W]
    v = p[:, 2 * W:3 * W]
    zwa = p[:, 3 * W:3 * W + 128]
    zg = p[:, 3 * W + 128:]
    lane = lax.broadcasted_iota(jnp.int32, zwa.shape, 1)
    tz = jnp.where(lane < 64, jnp.tanh(zwa), zwa)
    wa = _dot(tz.astype(BF16), wa_ref[...])
    logw = -RWKV_DECAY_SCALE * _sigmoid(w0_ref[...] + wa[:, :W])
    alpha = _sigmoid(a0_ref[...] + wa[:, W:])
    g = _dot(_sigmoid(zg).astype(BF16), gup_ref[...])

    hsum = hsum_ref[...]
    kk = k * kk_ref[...]
    kk = kk * lax.rsqrt(jnp.maximum(_dot_split(kk * kk, hsum), 1e-24))
    k2 = k * (1.0 + (alpha - 1.0) * ka_ref[...])
    bvec = kk * alpha

    cum = _dot_split_rhs(csum_ref[0], logw)
    cum_end = _dot_split_rhs(csum_ref[1], logw)
    e_neg = jnp.exp(-cum)
    e_rem = jnp.exp(cum_end - cum)
    at_ref[...] = -kk * jnp.exp(cum - logw)
    bt_ref[...] = bvec * e_neg
    kt_ref[...] = k2 * e_neg
    rt_ref[...] = r * jnp.exp(cum)
    bh_ref[...] = bvec * e_rem
    kh_ref[...] = k2 * e_rem
    v_ref[...] = v
    ec_ref[...] = jnp.exp(cum_end)

    def tile4(x):
        x4 = jnp.concatenate([x, x, x, x], axis=0)
        return jnp.where(masks_ref[_MASK_HEAD] > 0, x4, 0.0).astype(BF16)

    def keep(m, x):
        return jnp.where(masks_ref[m] > 0, x, 0.0)

    n = RWKV_HEADS * C

    group = RWKV_GROUP

    def prepare(gi, carry):
        cs = [gi * group + j for j in range(group)]
        rows = [pl.ds(pl.multiple_of(c * C, C), C) for c in cs]
        at = [tile4(at_ref[r, :]) for r in rows]
        vr = [tile4(v_ref[r, :]) for r in rows]
        a_ab, a_ak = [], []
        for j, (c, r) in enumerate(zip(cs, rows)):
            rt = tile4(rt_ref[r, :])
            lhs = jnp.concatenate([at[j], rt], axis=0)
            rhs = jnp.concatenate([tile4(bt_ref[r, :]), tile4(kt_ref[r, :])], axis=0)
            a = _dot_nt(lhs, rhs)
            a_ab.append(keep(_MASK_STRICT, a[:n, :n]))
            a_ak.append(keep(_MASK_STRICT, a[:n, n:]).astype(BF16))
            arb_ref[c] = keep(_MASK_INCL, a[n:, :n]).astype(BF16)
            arkv_ref[c] = _dot(keep(_MASK_INCL, a[n:, n:]).astype(BF16), vr[j])
            lhs_ref[c, n:2 * n, :] = rt
            vr_ref[c] = vr[j]
            upd_ref[c, 0:n, :] = tile4(bh_ref[r, :])
            upd_ref[c, n:2 * n, :] = tile4(kh_ref[r, :])
        akv = [_dot(a_ak[j], vr[j]).astype(BF16) for j in range(group)]
        t = [masks_ref[_MASK_EYE] + keep(_MASK_LEVEL0, a_ab[j]) for j in range(group)]
        for li in range(1, len(_INV_LEVELS)):
            tb = [t[j].astype(BF16) for j in range(group)]
            xt = [_dot(keep(_MASK_LEVEL0 + li, a_ab[j]).astype(BF16), tb[j]).astype(BF16) for j in range(group)]
            t = [t[j] + _dot(tb[j], xt[j]) for j in range(group)]
        for j, c in enumerate(cs):
            tb = t[j].astype(BF16)
            lhs_ref[c, 0:n, :] = _dot(tb, at[j]).astype(BF16)
            ut_ref[c] = _dot(tb, akv[j])
        return carry

    lax.fori_loop(0, ts // (C * group), prepare, 0)

    def advance(c, carry):
        r0 = pl.multiple_of(c * C, C)
        s = state_ref[...]
        g_us = _dot_nt(lhs_ref[c], s.astype(BF16))
        ub = (g_us[:n] + ut_ref[c]).astype(BF16)
        o = g_us[n:] + _dot(arb_ref[c], ub) + arkv_ref[c]
        oc_ref[pl.ds(r0, C), :] = o[0:C] + o[C:2 * C] + o[2 * C:3 * C] + o[3 * C:4 * C]
        upd = _dot_tn(jnp.concatenate([ub, vr_ref[c]], axis=0), upd_ref[c])
        state_ref[...] = s * ec_ref[pl.ds(r0, 1), :] + upd
        return carry

    lax.fori_loop(0, ts // C, advance, 0)

    o = oc_ref[...]
    hmean = hsum * (1.0 / HEAD_DIM)
    mean = _dot_split(o, hmean)
    d = o - mean
    var = _dot_split(d * d, hmean)
    o = d * lax.rsqrt(var + RWKV_LN_EPS) * lng_ref[...] + lnb_ref[...]
    bonus = _dot_split(r * k2 * rk_ref[...], hsum)
    o_ref[0] = ((o + bonus * v) * g).astype(o_ref.dtype)


def _rwkv(p, mu, w0, a0, wa_up, g_up, k_k, k_a, r_k, ln_g, ln_b):
    B, S, _ = p.shape
    ts = TOKEN_TILE
    W = RWKV_WIDTH
    n = RWKV_HEADS * RWKV_CHUNK
    nc = ts // RWKV_CHUNK
    masks = jnp.asarray(_rwkv_masks())
    csum = jnp.asarray(_chunk_sum_mats(ts), BF16)
    hsum = jnp.asarray(_head_sum_mat(W, HEAD_DIM), BF16)
    vec = _const_spec((1, W))
    buf = lambda: pltpu.VMEM((ts, W), F32)
    return pl.pallas_call(
        _rwkv_kernel,
        grid=(B, S // ts),
        in_specs=[pl.BlockSpec((1, ts, RWKV_COLS), lambda b, s: (b, s, 0)),
                  _const_spec((1, RWKV_COLS)), vec, vec, _const_spec((128, 2 * W)), _const_spec((128, W)),
                  vec, vec, vec, vec, vec,
                  _const_spec(masks.shape), _const_spec(csum.shape), _const_spec(hsum.shape)],
        out_specs=pl.BlockSpec((1, ts, W), lambda b, s: (b, s, 0)),
        out_shape=jax.ShapeDtypeStruct((B, S, W), BF16),
        scratch_shapes=[pltpu.VMEM((n, n), F32), pltpu.VMEM((1, RWKV_COLS), F32)] + [buf() for _ in range(9)]
                       + [pltpu.VMEM((nc, 2 * n, n), BF16), pltpu.VMEM((nc, 2 * n, n), BF16),
                          pltpu.VMEM((nc, n, n), BF16), pltpu.VMEM((nc, n, n), BF16),
                          pltpu.VMEM((nc, n, n), F32), pltpu.VMEM((nc, n, n), F32)],
        compiler_params=_params("parallel", "arbitrary"),
        name="rwkv7",
    )(p, mu, w0, a0, wa_up, g_up, k_k, k_a, r_k, ln_g, ln_b, masks, csum, hsum)


def _alibi_slopes(n):
    return 2.0 ** (-8.0 * np.arange(1, n + 1) / n)


def _dil_bias(group):
    window, dilation = DIL_PATTERNS[group]
    n = window // dilation
    slopes = _alibi_slopes(DIL_HEADS).reshape(DIL_GROUPS, -1)[group]
    rel = n + np.arange(n)[:, None] - np.arange(2 * n)[None, :]
    valid = (rel >= 0) & (rel <= n)
    bias = -slopes[:, None, None] * (dilation * rel).astype(np.float64)[None]
    return np.where(valid[None], bias, NEG_INF).astype(np.float32)


def _dil_attend(blocks, bias_of, front):
    shape = blocks[0][0].shape
    heads = shape[1] // HEAD_DIM
    head = lax.broadcasted_iota(jnp.int32, shape, 1) // HEAD_DIM
    pairs = [(b, h) for b in range(len(blocks)) for h in range(heads)]
    s = [_dot_nt(jnp.where(head == h, blocks[b][0], jnp.zeros(shape, BF16)), blocks[b][1]) + bias_of(h)
         for b, h in pairs]
    if front is not None:
        s = [jnp.where(front, NEG_INF, x) for x in s]
    m = [jnp.max(x, axis=-1, keepdims=True) for x in s]
    e = [jnp.exp2(x - mx) for x, mx in zip(s, m)]
    l = [jnp.sum(x, axis=-1, keepdims=True) for x in e]
    per_half = 128 // HEAD_DIM
    pv = [_dot(x.astype(BF16), blocks[b][2][:, (h // per_half) * 128:(h // per_half + 1) * 128])
          for x, (b, h) in zip(e, pairs)]
    low = lax.broadcasted_iota(jnp.int32, (shape[0], 128), 1) < HEAD_DIM
    results = []
    for b in range(len(blocks)):
        outs, lses = [], []
        for half in range(heads // per_half):
            i0, i1 = b * heads + half * per_half, b * heads + half * per_half + 1
            outs.append(jnp.where(low, pv[i0] / l[i0], pv[i1] / l[i1]))
            lses.append(jnp.where(low, m[i0] + jnp.log(l[i0]) * _LOG2E, m[i1] + jnp.log(l[i1]) * _LOG2E))
        results.append((outs, lses))
    return results


def _dil_kernel(c0_ref, c1_ref, c2_ref, p0_ref, p1_ref, p2_ref, b0_ref, b1_ref, b2_ref, y_ref, og_ref, lg_ref):
    n = DIL_BLOCK
    W = BRANCH_WIDTH
    sub = DIL_SUB
    first = pl.program_id(1) == 0
    front = jnp.logical_and(first, lax.broadcasted_iota(jnp.int32, (n, 2 * n), 1) < n)
    qscale = HEAD_DIM ** -0.5 * _LOG2E
    kcols, vcols, qcols = slice(0, W), slice(W, 2 * W), slice(2 * W, 3 * W)

    def scaled(q):
        return (q.astype(F32) * qscale).astype(BF16)

    def put(g, rows, out, lse):
        for half in range(W // 128):
            og_ref[g, half, rows, :] = out[half]
            lg_ref[g, half, rows, :] = lse[half]

    def block_of(prev, cur):
        return (scaled(cur[:, qcols]), jnp.concatenate([prev[:, kcols], cur[:, kcols]], axis=0),
                jnp.concatenate([prev[:, vcols], cur[:, vcols]], axis=0))

    (out, lse), = _dil_attend([block_of(p0_ref[0], c0_ref[0, 0:n, :])], lambda h: b0_ref[h], front)
    put(0, pl.ds(0, n), out, lse)
    per0 = DIL_G0_BLOCKS_PER_TRIP

    def g0_body(i, carry):
        blocks, rows = [], []
        for j in range(per0):
            r0 = pl.multiple_of((i * per0 + j) * n, n)
            kv = c0_ref[0, pl.ds(r0, 2 * n), 0:2 * W]
            blocks.append((scaled(c0_ref[0, pl.ds(r0 + n, n), qcols]), kv[:, kcols], kv[:, vcols]))
            rows.append(pl.ds(r0 + n, n))
        for r, (out, lse) in zip(rows, _dil_attend(blocks, lambda h: b0_ref[h], None)):
            put(0, r, out, lse)
        return carry

    assert (DIL_TILE // n - 1) % per0 == 0
    lax.fori_loop(0, (DIL_TILE // n - 1) // per0, g0_body, 0)

    d1 = DIL_PATTERNS[1][1]

    def g1_rows(u, r):
        return pl.ds(pl.multiple_of(u * sub + r * n, n), n)

    def g1_run(u, prevs, front_mask):
        blocks = [block_of(prevs[r], c1_ref[0, g1_rows(u, r), :]) for r in range(d1)]
        for r, (out, lse) in enumerate(_dil_attend(blocks, lambda h: b1_ref[h], front_mask)):
            put(1, pl.ds(u * sub + r, n, stride=d1), out, lse)

    g1_run(0, [p1_ref[0, r * n:(r + 1) * n, :] for r in range(d1)], front)

    def g1_body(u, carry):
        g1_run(u, [c1_ref[0, g1_rows(u - 1, r), 0:2 * W] for r in range(d1)], None)
        return carry

    lax.fori_loop(1, DIL_TILE // sub, g1_body, 0)

    d2 = DIL_PATTERNS[2][1]
    piece = sub // d2
    per2 = DIL_G2_BLOCKS_PER_TRIP

    def g2_body(i, carry):
        def rows_of(ref, r, cols):
            return jnp.concatenate(
                [ref[0, pl.ds(pl.multiple_of(u * sub + r * piece, piece), piece), cols]
                 for u in range(DIL_TILE // sub)], axis=0)
        rs = [i * per2 + j for j in range(per2)]
        blocks = [block_of(rows_of(p2_ref, r, slice(0, 2 * W)), rows_of(c2_ref, r, slice(0, 3 * W))) for r in rs]
        for r, (out, lse) in zip(rs, _dil_attend(blocks, lambda h: b2_ref[h], front)):
            put(2, pl.ds(r, n, stride=d2), out, lse)
        return carry

    assert d2 % per2 == 0
    lax.fori_loop(0, d2 // per2, g2_body, 0)

    rows_per = 256

    def combine(j, carry):
        rows = pl.ds(pl.multiple_of(j * rows_per, rows_per), rows_per)
        for half in range(W // 128):
            ls = [lg_ref[g, half, rows, :] for g in range(DIL_GROUPS)]
            m = jnp.maximum(jnp.maximum(ls[0], ls[1]), ls[2])
            es = [jnp.exp2(x - m) for x in ls]
            num = es[0] * og_ref[0, half, rows, :] + es[1] * og_ref[1, half, rows, :] + es[2] * og_ref[2, half, rows, :]
            y_ref[0, rows, half * 128:(half + 1) * 128] = (num / (es[0] + es[1] + es[2])).astype(y_ref.dtype)
        return carry

    lax.fori_loop(0, DIL_TILE // rows_per, combine, 0)


def _dilated(pd0, pd1, pd2):
    B, S, _ = pd0.shape
    W = BRANCH_WIDTH
    T = DIL_TILE
    assert S % T == 0
    for (window, d) in DIL_PATTERNS:
        assert window // d == DIL_BLOCK
    assert DIL_PATTERNS[0][1] == 1 and DIL_PATTERNS[1][1] * DIL_BLOCK == DIL_SUB and DIL_PATTERNS[2][1] * DIL_BLOCK == T
    biases = [jnp.asarray(_dil_bias(g) * _LOG2E) for g in range(DIL_GROUPS)]
    cur = pl.BlockSpec((1, T, 3 * W), lambda b, i: (b, i, 0))
    prev = lambda rows: pl.BlockSpec((1, rows, 2 * W), lambda b, i: (b, jnp.maximum(i * (T // rows) - 1, 0), 0))
    return pl.pallas_call(
        _dil_kernel,
        grid=(B, S // T),
        in_specs=[cur, cur, cur, prev(DIL_BLOCK), prev(DIL_SUB), prev(T)] + [_const_spec(b.shape) for b in biases],
        out_specs=pl.BlockSpec((1, T, W), lambda b, i: (b, i, 0)),
        out_shape=jax.ShapeDtypeStruct((B, S, W), BF16),
        scratch_shapes=[pltpu.VMEM((DIL_GROUPS, W // 128, T, 128), F32)] * 2,
        compiler_params=_params("parallel", "arbitrary"),
        name="dilated",
    )(pd0, pd1, pd2, pd0, pd1, pd2, *biases)


def _diff_bias():
    t = DIFF_TILE
    slopes = _alibi_slopes(DIFF_HEADS) * _LOG2E
    rel = (np.arange(t)[None, :] - np.arange(t)[:, None]).astype(np.float64)
    off = -slopes[:, None, None] * rel[None]
    diag = np.where(rel[None] >= 0, off, NEG_INF)
    return off.astype(np.float32), diag.astype(np.float32)


def _diff_kernel(q_ref, k_ref, vt_ref, boff_ref, bdiag_ref, lamv_ref, gain_ref, hsum_ref, o_ref, m_ref, acc_ref,
                 s_ref, *, lambda_init):
    t = DIFF_TILE
    n_maps = 2 * DIFF_HEADS
    ones_rows = DIFF_ACC_ROWS - HEAD_DIM
    qi = pl.program_id(1)
    q = (q_ref[0].astype(F32) * (DIFF_QK_DIM ** -0.5 * _LOG2E)).astype(BF16)
    lane = lax.broadcasted_iota(jnp.int32, q.shape, 1)
    qs = [jnp.where(lane // DIFF_QK_DIM == c, q, jnp.zeros_like(q)) for c in range(n_maps)]
    slopes = _alibi_slopes(DIFF_HEADS) * _LOG2E

    m_ref[...] = jnp.full(m_ref.shape, NEG_INF, F32)
    acc_ref[...] = jnp.zeros_like(acc_ref)

    def keys(kb):
        return pl.ds(pl.multiple_of(kb * t, t), t)

    def fold(kb, slot, bias_ref, k_next):
        ones = jnp.ones((ones_rows, t), BF16)
        vt = [jnp.concatenate([vt_ref[h * HEAD_DIM:(h + 1) * HEAD_DIM, keys(kb)], ones], axis=0)
              for h in range(DIFF_HEADS)]
        shift = ((qi - kb) * t).astype(F32)
        for c in range(n_maps):
            if k_next is not None:
                s_ref[1 - slot, c] = _dot_nt(k_next, qs[c])
            block_bias = float(slopes[c // 2]) * shift
            s = s_ref[slot, c] + bias_ref[c // 2]
            m_prev = m_ref[c, 0:1, :]
            m_new = jnp.maximum(m_prev, jnp.max(s, axis=0, keepdims=True) - block_bias)
            e = jnp.exp2(s - (m_new + block_bias)).astype(BF16)
            m_ref[c] = jnp.broadcast_to(m_new, m_ref.shape[1:])
            acc_ref[c] = acc_ref[c] * jnp.exp2(m_prev - m_new) + _dot(vt[c // 2], e)

    k0 = k_ref[0, keys(0), :]
    for c in range(n_maps):
        s_ref[0, c] = _dot_nt(k0, qs[c])

    def pair(jj, carry):
        fold(2 * jj, 0, boff_ref, k_ref[0, keys(2 * jj + 1), :])
        fold(2 * jj + 1, 1, boff_ref, k_ref[0, keys(2 * jj + 2), :])
        return carry

    lax.fori_loop(0, qi // 2, pair, 0)

    @pl.when(qi % 2 == 1)
    def _():
        fold(qi - 1, 0, boff_ref, k_ref[0, keys(qi), :])
        fold(qi, 1, bdiag_ref, None)

    @pl.when(qi % 2 == 0)
    def _():
        fold(qi, 0, bdiag_ref, None)

    lamv = lamv_ref[...]
    lam = (jnp.exp(jnp.sum(lamv[0:1] * lamv[1:2], axis=-1, keepdims=True))
           - jnp.exp(jnp.sum(lamv[2:3] * lamv[3:4], axis=-1, keepdims=True)) + lambda_init)
    heads = []
    for h in range(DIFF_HEADS):
        a0 = acc_ref[2 * h]
        a1 = acc_ref[2 * h + 1]
        heads.append(a0[0:HEAD_DIM] / a0[HEAD_DIM:HEAD_DIM + 1]
                     - lam * (a1[0:HEAD_DIM] / a1[HEAD_DIM:HEAD_DIM + 1]))
    o = jnp.concatenate(heads, axis=0).T
    ms = _dot_split(o * o, hsum_ref[...]) * (1.0 / HEAD_DIM)
    o = o * lax.rsqrt(ms + DIFF_SUBLN_EPS) * gain_ref[...]
    o_ref[0] = (o * (1.0 - lambda_init)).astype(o_ref.dtype)


def _diff_attention(p_qk, p_vt, lamv, subln_g, lambda_init):
    B, S, _ = p_qk.shape
    t = DIFF_TILE
    W = BRANCH_WIDTH
    boff, bdiag = (jnp.asarray(b) for b in _diff_bias())
    hsum = jnp.asarray(_head_sum_mat(W, HEAD_DIM), BF16)
    gain = jnp.tile(subln_g.reshape(1, HEAD_DIM), (1, DIFF_HEADS))
    single = lambda shape: pl.BlockSpec(shape, lambda b, i: (0,) * len(shape), pipeline_mode=pl.Buffered(1))
    return pl.pallas_call(
        functools.partial(_diff_kernel, lambda_init=lambda_init),
        grid=(B, S // t),
        in_specs=[pl.BlockSpec((1, t, W), lambda b, i: (b, i, 0)), pl.BlockSpec((1, S, W), lambda b, i: (b, 0, 1)),
                  pl.BlockSpec((W, S), lambda b, i: (0, b)),
                  single(boff.shape), single(bdiag.shape), _const_spec(lamv.shape), _const_spec((1, W)),
                  _const_spec((W, W))],
        out_specs=pl.BlockSpec((1, t, W), lambda b, i: (b, i, 0)),
        out_shape=jax.ShapeDtypeStruct((B, S, W), BF16),
        scratch_shapes=[pltpu.VMEM((2 * DIFF_HEADS, 8, t), F32), pltpu.VMEM((2 * DIFF_HEADS, DIFF_ACC_ROWS, t), F32),
                        pltpu.VMEM((2, 2 * DIFF_HEADS, t, t), F32)],
        compiler_params=_params("parallel", "arbitrary"),
        name="diffattn",
    )(p_qk, p_qk, p_vt, boff, bdiag, lamv, gain, hsum)


def _conv_kernel(p_ref, w_ref, b_ref, g_ref, beta_ref, o_ref, u_ref):
    ts = p_ref.shape[1]
    Cc = CONV_CHANNELS
    halo = CONV_HALO
    rows = 64

    @pl.when(pl.program_id(1) == 0)
    def _():
        u_ref[0:halo, :] = jnp.zeros((halo, Cc), F32)

    @pl.when(pl.program_id(1) > 0)
    def _():
        u_ref[0:halo, :] = u_ref[ts:ts + halo, :]

    p = p_ref[0]
    u_ref[halo:halo + ts, :] = p[:, :Cc] * _sigmoid(p[:, Cc:])

    def block(i, carry):
        r0 = pl.multiple_of(i * rows, rows)
        acc = jnp.zeros((rows, Cc), F32) + b_ref[...]
        win = u_ref[pl.ds(r0, rows + halo), :]
        first_tap = halo - CONV_WIDTH + 1
        for b in range(8):
            wb = win if b == 0 else pltpu.roll(win, rows + halo - b, 0)
            for a in range(halo // 8 + 1):
                j = 8 * a + b - first_tap
                if 0 <= j < CONV_WIDTH:
                    acc = acc + w_ref[j:j + 1, :] * wb[8 * a:8 * a + rows]
        mu = jnp.mean(acc, axis=-1, keepdims=True)
        d = acc - mu
        var = jnp.mean(d * d, axis=-1, keepdims=True)
        y = d * lax.rsqrt(var + CONV_LN_EPS) * g_ref[...] + beta_ref[...]
        o_ref[0, pl.ds(r0, rows), :] = (y * _sigmoid(y)).astype(o_ref.dtype)
        return carry

    lax.fori_loop(0, ts // rows, block, 0)


def _conformer_conv(p_conv, dw_w, dw_b, ln_g, ln_b):
    B, S, _ = p_conv.shape
    ts = TOKEN_TILE
    Cc = CONV_CHANNELS
    vec = _const_spec((1, Cc))
    return pl.pallas_call(
        _conv_kernel,
        grid=(B, S // ts),
        in_specs=[pl.BlockSpec((1, ts, CONV_COLS), lambda b, s: (b, s, 0)), _const_spec((CONV_WIDTH, Cc)),
                  vec, vec, vec],
        out_specs=pl.BlockSpec((1, ts, Cc), lambda b, s: (b, s, 0)),
        out_shape=jax.ShapeDtypeStruct((B, S, Cc), BF16),
        scratch_shapes=[pltpu.VMEM((ts + CONV_HALO, Cc), F32)],
        compiler_params=_params("parallel", "arbitrary"),
        name="conformer_conv",
    )(p_conv, dw_w, dw_b, ln_g, ln_b)


def _merge_kernel(x_ref, ya_ref, yb_ref, yc_ref, yd_ref,
                  gpre_ref, gpost_ref, wg_ref, gb_ref, wb_ref, wo_ref, out_ref):
    x = x_ref[...]
    h = _rms(x, gpre_ref[...]).astype(BF16)
    ys = (ya_ref[...], yb_ref[...], yc_ref[...], yd_ref[...])
    merged = None
    for n in range(N_BRANCHES):
        gate = _sigmoid(_dot(h, wg_ref[n]) + gb_ref[n])
        term = gate * _dot(ys[n], wb_ref[n])
        merged = term if merged is None else merged + term
    y = _dot(merged.astype(BF16), wo_ref[...])
    out_ref[...] = x + _rms(y, gpost_ref[...])


def _merge(x2, ya, yb, yc, yd, gpre, gpost, w_gate, gate_bias, w_branch, w_out):
    T = x2.shape[0]
    tm = TOKEN_TILE
    D = D_MODEL
    W = BRANCH_WIDTH
    row = lambda w: pl.BlockSpec((tm, w), lambda i: (i, 0))
    single = lambda shape: pl.BlockSpec(shape, lambda i: (0,) * len(shape), pipeline_mode=pl.Buffered(1))
    return pl.pallas_call(
        _merge_kernel,
        grid=(T // tm,),
        in_specs=[row(D)] + [row(W)] * 4 + [_const_spec((1, D)), _const_spec((1, D)),
                  single((N_BRANCHES, D, D)), _const_spec((N_BRANCHES, 1, D)), single((N_BRANCHES, W, D)),
                  single((D, D))],
        out_specs=row(D),
        out_shape=jax.ShapeDtypeStruct((T, D), F32),
        compiler_params=_params("parallel"),
        name="merge",
    )(x2, ya, yb, yc, yd, gpre, gpost, w_gate, gate_bias, w_branch, w_out)


def _ffn_kernel(x_ref, gpre_ref, gpost_ref, wg_ref, wu_ref, wd_ref, out_ref):
    x = x_ref[...]
    h = _rms(x, gpre_ref[...]).astype(BF16)
    g = _dot(h, wg_ref[...])
    u = _dot(h, wu_ref[...])
    a = (g * _sigmoid(g) * u).astype(BF16)
    f = _dot(a, wd_ref[...])
    out_ref[...] = x + _rms(f, gpost_ref[...])


def _ffn(x2, gpre, gpost, w_gate, w_up, w_down):
    T = x2.shape[0]
    tm = TOKEN_TILE
    D = D_MODEL
    row = pl.BlockSpec((tm, D), lambda i: (i, 0))
    single = lambda shape: pl.BlockSpec(shape, lambda i: (0,) * len(shape), pipeline_mode=pl.Buffered(1))
    return pl.pallas_call(
        _ffn_kernel,
        grid=(T // tm,),
        in_specs=[row, _const_spec((1, D)), _const_spec((1, D)), single((D, D_FF)), single((D, D_FF)),
                  single((D_FF, D))],
        out_specs=row,
        out_shape=jax.ShapeDtypeStruct((T, D), F32),
        compiler_params=_params("parallel"),
        name="swiglu",
    )(x2, gpre, gpost, w_gate, w_up, w_down)


def _layer(x2, B, S, l, prm):
    D = D_MODEL
    W = BRANCH_WIDTH
    row = lambda a: a.reshape(1, -1)
    w_in = prm["w_in"][l]
    w_dil = w_in[:, RWKV_COLS:RWKV_COLS + DIL_COLS].reshape(D, 3, DIL_GROUPS, W)
    w_dil = jnp.concatenate([w_dil[:, 1], w_dil[:, 2], w_dil[:, 0]], axis=-1).transpose(1, 0, 2)
    diff0 = RWKV_COLS + DIL_COLS
    w_mix = jnp.concatenate([w_in[:, :RWKV_COLS]] + [w_dil[g] for g in range(DIL_GROUPS)]
                            + [w_in[:, diff0:diff0 + DIFF_QK_COLS], w_in[:, diff0 + DIFF_COLS:MIX_COLS]],
                            axis=1).astype(BF16)
    w_vt = w_in[:, diff0 + DIFF_QK_COLS:diff0 + DIFF_COLS].T.astype(BF16)
    w_gate = w_in[:, MIX_COLS:].reshape(D, N_BRANCHES, D).transpose(1, 0, 2).astype(BF16)

    p_rwkv, pd0, pd1, pd2, p_qk, p_conv, p_vt = _inproj(x2, row(prm["norm_mix_pre"][l]), w_mix, w_vt)

    zeros = jnp.zeros((64, W), F32)
    wa_up = jnp.concatenate([jnp.concatenate([prm["rwkv_w_up"][l], zeros], axis=1),
                             jnp.concatenate([zeros, prm["rwkv_a_up"][l]], axis=1)], axis=0).astype(BF16)
    ya = _rwkv(p_rwkv.reshape(B, S, RWKV_COLS), row(prm["rwkv_mu"][l]), row(prm["rwkv_w0"][l]),
               row(prm["rwkv_a0"][l]), wa_up, prm["rwkv_g_up"][l].astype(BF16), row(prm["rwkv_k_k"][l]),
               row(prm["rwkv_k_a"][l]), row(prm["rwkv_r_k"][l]), row(prm["rwkv_ln_g"][l]),
               row(prm["rwkv_ln_b"][l]))

    group_w = DIL_COLS // DIL_GROUPS
    yb = _dilated(pd0.reshape(B, S, group_w), pd1.reshape(B, S, group_w), pd2.reshape(B, S, group_w))

    lambda_init = 0.8 - 0.6 * math.exp(-0.3 * l)
    lamv = jnp.stack([prm["diff_lam_q1"][l], prm["diff_lam_k1"][l], prm["diff_lam_q2"][l], prm["diff_lam_k2"][l]])
    yc = _diff_attention(p_qk.reshape(B, S, DIFF_QK_COLS), p_vt, lamv, prm["diff_subln_g"][l], lambda_init)

    yd = _conformer_conv(p_conv.reshape(B, S, CONV_COLS), prm["conv_dw_w"][l].reshape(CONV_WIDTH, CONV_CHANNELS),
                         row(prm["conv_dw_b"][l]), row(prm["conv_ln_g"][l]), row(prm["conv_ln_b"][l]))

    x2 = _merge(x2, ya.reshape(B * S, W), yb.reshape(B * S, W), yc.reshape(B * S, W), yd.reshape(B * S, W),
                row(prm["norm_mix_pre"][l]), row(prm["norm_mix_post"][l]), w_gate,
                prm["gate_bias"][l].reshape(N_BRANCHES, 1, D), prm["w_branch"][l].astype(BF16),
                prm["w_out"][l].astype(BF16))

    return _ffn(x2, row(prm["norm_ffn_pre"][l]), row(prm["norm_ffn_post"][l]), prm["ffn_w_gate"][l].astype(BF16),
                prm["ffn_w_up"][l].astype(BF16), prm["ffn_w_down"][l].astype(BF16))


def kernel(x, norm_mix_pre, norm_mix_post, norm_ffn_pre, norm_ffn_post, w_in, gate_bias, rwkv_mu, rwkv_w0, rwkv_w_up, rwkv_a0, rwkv_a_up, rwkv_g_up, rwkv_k_k, rwkv_k_a, rwkv_r_k, rwkv_ln_g, rwkv_ln_b, diff_lam_q1, diff_lam_k1, diff_lam_q2, diff_lam_k2, diff_subln_g, conv_dw_w, conv_dw_b, conv_ln_g, conv_ln_b, w_branch, w_out, ffn_w_gate, ffn_w_up, ffn_w_down):
    prm = dict(locals())
    B, S, D = x.shape
    x2 = x.reshape(B * S, D)
    for l in range(w_in.shape[0]):
        x2 = _layer(x2, B, S, l, prm)
    return x2.reshape(B, S, D)
```

```python
import functools
import math

import numpy as np
import jax
import jax.numpy as jnp
from jax import lax
from jax.experimental import pallas as pl
from jax.experimental.pallas import tpu as pltpu

F32 = jnp.float32
BF16 = jnp.bfloat16

D_MODEL = 1024
DEPTH = 4
HEAD_DIM = 64
BRANCH_WIDTH = 256
N_BRANCHES = 4

RWKV_HEADS = 4
RWKV_WIDTH = 256
RWKV_COLS = 1024
RWKV_DECAY_SCALE = 0.606531
RWKV_LN_EPS = 64e-5
RWKV_CHUNK = 64
RWKV_TILE = 256
RWKV_SEQS = 2

DIL_PATTERNS = ((128, 1), (512, 4), (2048, 16))
DIL_GROUPS = 3
DIL_HEADS = 12
DIL_QKV = 768
DIL_COLS = 2304
DIL_BLOCK = 128
DIL_TILE = 2048
DIL_SUB = 512
DIL_G0_BLOCKS_PER_TRIP = 3
DIL_G2_BLOCKS_PER_TRIP = 4

DIFF_HEADS = 4
DIFF_QK_DIM = 32
DIFF_COLS = 768
DIFF_TILE = 512
DIFF_QK_COLS = 512
DIFF_ACC_ROWS = HEAD_DIM + 16
DIFF_SUBLN_EPS = 1e-5

CONV_CHANNELS = 256
CONV_WIDTH = 31
CONV_COLS = 512
CONV_LN_EPS = 1e-5
CONV_HALO = 32

MIX_COLS = RWKV_COLS + DIL_COLS + DIFF_COLS + CONV_COLS
D_FF = 2816
NORM_EPS = 1e-6
NEG_INF = -1e30
_LOG2E = 1.4426950408889634

VMEM_LIMIT_BYTES = 56 * 1024 * 1024
TOKEN_TILE = 512
MXU_COLS = 256


def _dot(a, b):
    return jnp.dot(a, b, preferred_element_type=F32)


def _dot_nt(a, b):
    return lax.dot_general(a, b, (((1,), (1,)), ((), ())), preferred_element_type=F32)


def _dot_tn(a, b):
    return lax.dot_general(a, b, (((0,), (0,)), ((), ())), preferred_element_type=F32)


def _dot_split(x, m):
    hi = x.astype(BF16)
    lo = (x - hi.astype(F32)).astype(BF16)
    return _dot(hi, m) + _dot(lo, m)


def _dot_split_rhs(m, x):
    hi = x.astype(BF16)
    lo = (x - hi.astype(F32)).astype(BF16)
    return _dot(m, hi) + _dot(m, lo)


def _rms(x, gain):
    return x * lax.rsqrt(jnp.mean(x * x, axis=-1, keepdims=True) + NORM_EPS) * gain


def _sigmoid(x):
    return 1.0 / (1.0 + jnp.exp(-x))


def _const_spec(shape):
    nd = len(shape)
    return pl.BlockSpec(shape, lambda *_: (0,) * nd)


def _params(*sem):
    return pltpu.CompilerParams(dimension_semantics=sem, vmem_limit_bytes=VMEM_LIMIT_BYTES)


def _inproj_kernel(x_ref, g_ref, w_ref, wvt_ref, o_rwkv, o_dil0, o_dil1, o_dil2, o_diff, o_conv, o_vt, h_ref):
    tm = x_ref.shape[0]
    hf = _rms(x_ref[...], g_ref[...])
    h = hf.astype(BF16)
    slabs = D_MODEL // 128
    for j in range(slabs):
        h_ref[j] = hf[:, j * 128:(j + 1) * 128]

    def permuted(d):
        cols = [jnp.concatenate([h_ref[j, pl.ds(r, tm // d, stride=d), :] for r in range(d)], axis=0)
                for j in range(slabs)]
        return jnp.concatenate(cols, axis=1).astype(BF16)

    group_w = DIL_COLS // DIL_GROUPS
    col = 0
    for o_ref, width, lhs in ((o_rwkv, RWKV_COLS, h), (o_dil0, group_w, h),
                              (o_dil1, group_w, permuted(DIL_PATTERNS[1][1])),
                              (o_dil2, group_w, permuted(DIL_PATTERNS[2][1])),
                              (o_diff, DIFF_QK_COLS, h), (o_conv, CONV_COLS, h)):
        for c0 in range(0, width, MXU_COLS):
            y = _dot(lhs, w_ref[:, col + c0:col + c0 + MXU_COLS])
            o_ref[:, c0:c0 + MXU_COLS] = y.astype(o_ref.dtype)
        col += width
    o_vt[...] = _dot_nt(wvt_ref[...], h).astype(o_vt.dtype)


def _inproj(x2, gain, w_mix, w_vt):
    T = x2.shape[0]
    tm = TOKEN_TILE
    assert tm == DIL_SUB
    group_w = DIL_COLS // DIL_GROUPS
    W = BRANCH_WIDTH
    row = lambda w: pl.BlockSpec((tm, w), lambda i: (i, 0))
    bf = lambda w: jax.ShapeDtypeStruct((T, w), BF16)
    return pl.pallas_call(
        _inproj_kernel,
        grid=(T // tm,),
        in_specs=[row(D_MODEL), _const_spec((1, D_MODEL)), _const_spec((D_MODEL, MIX_COLS - W)),
                  _const_spec((W, D_MODEL))],
        out_specs=[row(RWKV_COLS), row(group_w), row(group_w), row(group_w), row(DIFF_QK_COLS), row(CONV_COLS),
                   pl.BlockSpec((W, tm), lambda i: (0, i))],
        out_shape=[jax.ShapeDtypeStruct((T, RWKV_COLS), F32), bf(group_w), bf(group_w), bf(group_w),
                   bf(DIFF_QK_COLS), jax.ShapeDtypeStruct((T, CONV_COLS), F32),
                   jax.ShapeDtypeStruct((W, T), BF16)],
        scratch_shapes=[pltpu.VMEM((D_MODEL // 128, tm, 128), F32)],
        compiler_params=_params("parallel"),
        name="inproj",
    )(x2, gain, w_mix, w_vt)


_MASK_STRICT, _MASK_INCL, _MASK_EYE, _MASK_LEVEL0 = 0, 1, 2, 3
_INV_LEVELS = (1, 2, 4, 8, 16, 32)


def _rwkv_masks():
    n = RWKV_HEADS * RWKV_CHUNK
    r = np.arange(n)[:, None]
    c = np.arange(n)[None, :]
    same = (r // RWKV_CHUNK) == (c // RWKV_CHUNK)
    out = [same & (r > c), same & (r >= c), r == c]
    for s in _INV_LEVELS:
        out.append((r // (2 * s) == c // (2 * s)) & (r % (2 * s) >= s) & (c % (2 * s) < s))
    head = (r // RWKV_CHUNK) == (c // HEAD_DIM)
    out.append(head)
    return np.stack(out).astype(np.float32)


_MASK_HEAD = 3 + len(_INV_LEVELS)


def _chunk_sum_mats(ts):
    r = np.arange(ts)[:, None]
    c = np.arange(ts)[None, :]
    same = (r // RWKV_CHUNK) == (c // RWKV_CHUNK)
    return np.stack([same & (c <= r), same]).astype(np.float32)


def _head_sum_mat(width, group):
    r = np.arange(width)[:, None]
    c = np.arange(width)[None, :]
    return ((r // group) == (c // group)).astype(np.float32)


def _rwkv_kernel(p_ref, mu_ref, w0_ref, a0_ref, wa_ref, gup_ref, kk_ref, ka_ref, rk_ref, lng_ref, lnb_ref,
                 masks_ref, csum_ref, hsum_ref, o_ref,
                 state_ref, prev_ref, at_ref, bt_ref, kt_ref, rt_ref, bh_ref, kh_ref, v_ref, ec_ref, oc_ref,
                 lhs_ref, upd_ref, vr_ref, arb_ref, ut_ref, arkv_ref):
    nseq, ts = p_ref.shape[0], p_ref.shape[1]
    W = RWKV_WIDTH
    C = RWKV_CHUNK
    n = RWKV_HEADS * C
    nc = ts // C

    @pl.when(pl.program_id(1) == 0)
    def _():
        state_ref[...] = jnp.zeros_like(state_ref)
        prev_ref[...] = jnp.zeros_like(prev_ref)

    hsum = hsum_ref[...]
    kept = []
    for q in range(nseq):
        p = p_ref[q]
        row = lax.broadcasted_iota(jnp.int32, p.shape, 0)
        p_prev = jnp.where(row == 0, prev_ref[q], pltpu.roll(p, 1, 0))
        prev_ref[q] = p[ts - 1:ts, :]
        p = p + (p_prev - p) * mu_ref[...]

        r = p[:, 0:W]
        k = p[:, W:2 * W]
        v = p[:, 2 * W:3 * W]
        zwa = p[:, 3 * W:3 * W + 128]
        zg = p[:, 3 * W + 128:]
        lane = lax.broadcasted_iota(jnp.int32, zwa.shape, 1)
        tz = jnp.where(lane < 64, jnp.tanh(zwa), zwa)
        wa = _dot(tz.astype(BF16), wa_ref[...])
        logw = -RWKV_DECAY_SCALE * _sigmoid(w0_ref[...] + wa[:, :W])
        alpha = _sigmoid(a0_ref[...] + wa[:, W:])
        g = _dot(_sigmoid(zg).astype(BF16), gup_ref[...])

        kk = k * kk_ref[...]
        kk = kk * lax.rsqrt(jnp.maximum(_dot_split(kk * kk, hsum), 1e-24))
        k2 = k * (1.0 + (alpha - 1.0) * ka_ref[...])
        bvec = kk * alpha

        cum = _dot_split_rhs(csum_ref[0], logw)
        cum_end = _dot_split_rhs(csum_ref[1], logw)
        e_neg = jnp.exp(-cum)
        e_rem = jnp.exp(cum_end - cum)
        at_ref[q] = -kk * jnp.exp(cum - logw)
        bt_ref[q] = bvec * e_neg
        kt_ref[q] = k2 * e_neg
        rt_ref[q] = r * jnp.exp(cum)
        bh_ref[q] = bvec * e_rem
        kh_ref[q] = k2 * e_rem
        v_ref[q] = v
        ec_ref[q] = jnp.exp(cum_end)
        kept.append((r * k2 * rk_ref[...], v, g))

    def tile4(x):
        x4 = jnp.concatenate([x, x, x, x], axis=0)
        return jnp.where(masks_ref[_MASK_HEAD] > 0, x4, 0.0).astype(BF16)

    def keep(m, x):
        return jnp.where(masks_ref[m] > 0, x, 0.0)

    chains = [(q, c) for q in range(nseq) for c in range(nc)]
    rows = [slice(c * C, (c + 1) * C) for _, c in chains]
    at = [tile4(at_ref[q, r, :]) for (q, _), r in zip(chains, rows)]
    vr = [tile4(v_ref[q, r, :]) for (q, _), r in zip(chains, rows)]
    a_ab, a_ak = [], []
    for j, ((q, c), r) in enumerate(zip(chains, rows)):
        rt = tile4(rt_ref[q, r, :])
        lhs = jnp.concatenate([at[j], rt], axis=0)
        rhs = jnp.concatenate([tile4(bt_ref[q, r, :]), tile4(kt_ref[q, r, :])], axis=0)
        a = _dot_nt(lhs, rhs)
        a_ab.append(keep(_MASK_STRICT, a[:n, :n]))
        a_ak.append(keep(_MASK_STRICT, a[:n, n:]).astype(BF16))
        arb_ref[q, c] = keep(_MASK_INCL, a[n:, :n]).astype(BF16)
        arkv_ref[q, c] = _dot(keep(_MASK_INCL, a[n:, n:]).astype(BF16), vr[j])
        lhs_ref[q, c, n:2 * n, :] = rt
        vr_ref[q, c] = vr[j]
        upd_ref[q, c, 0:n, :] = tile4(bh_ref[q, r, :])
        upd_ref[q, c, n:2 * n, :] = tile4(kh_ref[q, r, :])
    akv = [_dot(a_ak[j], vr[j]).astype(BF16) for j in range(len(chains))]
    t = [masks_ref[_MASK_EYE] + keep(_MASK_LEVEL0, a) for a in a_ab]
    for li in range(1, len(_INV_LEVELS)):
        tb = [x.astype(BF16) for x in t]
        xt = [_dot(keep(_MASK_LEVEL0 + li, a).astype(BF16), y).astype(BF16) for a, y in zip(a_ab, tb)]
        t = [x + _dot(y, z) for x, y, z in zip(t, tb, xt)]
    for j, (q, c) in enumerate(chains):
        tb = t[j].astype(BF16)
        lhs_ref[q, c, 0:n, :] = _dot(tb, at[j]).astype(BF16)
        ut_ref[q, c] = _dot(tb, akv[j])

    for c in range(nc):
        s = [state_ref[q] for q in range(nseq)]
        g_us = [_dot_nt(lhs_ref[q, c], s[q].astype(BF16)) for q in range(nseq)]
        ub = [(g_us[q][:n] + ut_ref[q, c]).astype(BF16) for q in range(nseq)]
        o = [g_us[q][n:] + _dot(arb_ref[q, c], ub[q]) + arkv_ref[q, c] for q in range(nseq)]
        upd = [_dot_tn(jnp.concatenate([ub[q], vr_ref[q, c]], axis=0), upd_ref[q, c]) for q in range(nseq)]
        for q in range(nseq):
            oc_ref[q, c * C:(c + 1) * C, :] = o[q][0:C] + o[q][C:2 * C] + o[q][2 * C:3 * C] + o[q][3 * C:4 * C]
            state_ref[q] = s[q] * ec_ref[q, c * C:c * C + 1, :] + upd[q]

    hmean = hsum * (1.0 / HEAD_DIM)
    for q, (rk2, v, g) in enumerate(kept):
        o = oc_ref[q]
        mean = _dot_split(o, hmean)
        d = o - mean
        var = _dot_split(d * d, hmean)
        o = d * lax.rsqrt(var + RWKV_LN_EPS) * lng_ref[...] + lnb_ref[...]
        bonus = _dot_split(rk2, hsum)
        o_ref[q] = ((o + bonus * v) * g).astype(o_ref.dtype)


def _rwkv(p, mu, w0, a0, wa_up, g_up, k_k, k_a, r_k, ln_g, ln_b):
    B, S, _ = p.shape
    ts = RWKV_TILE
    nseq = RWKV_SEQS
    assert B % nseq == 0 and S % ts == 0
    W = RWKV_WIDTH
    n = RWKV_HEADS * RWKV_CHUNK
    nc = ts // RWKV_CHUNK
    masks = jnp.asarray(_rwkv_masks())
    csum = jnp.asarray(_chunk_sum_mats(ts), BF16)
    hsum = jnp.asarray(_head_sum_mat(W, HEAD_DIM), BF16)
    vec = _const_spec((1, W))
    buf = lambda: pltpu.VMEM((nseq, ts, W), F32)
    return pl.pallas_call(
        _rwkv_kernel,
        grid=(B // nseq, S // ts),
        in_specs=[pl.BlockSpec((nseq, ts, RWKV_COLS), lambda b, s: (b, s, 0)),
                  _const_spec((1, RWKV_COLS)), vec, vec, _const_spec((128, 2 * W)), _const_spec((128, W)),
                  vec, vec, vec, vec, vec,
                  _const_spec(masks.shape), _const_spec(csum.shape), _const_spec(hsum.shape)],
        out_specs=pl.BlockSpec((nseq, ts, W), lambda b, s: (b, s, 0)),
        out_shape=jax.ShapeDtypeStruct((B, S, W), BF16),
        scratch_shapes=[pltpu.VMEM((nseq, n, n), F32), pltpu.VMEM((nseq, 1, RWKV_COLS), F32)]
                       + [buf() for _ in range(9)]
                       + [pltpu.VMEM((nseq, nc, 2 * n, n), BF16), pltpu.VMEM((nseq, nc, 2 * n, n), BF16),
                          pltpu.VMEM((nseq, nc, n, n), BF16), pltpu.VMEM((nseq, nc, n, n), BF16),
                          pltpu.VMEM((nseq, nc, n, n), F32), pltpu.VMEM((nseq, nc, n, n), F32)],
        compiler_params=_params("parallel", "arbitrary"),
        name="rwkv7",
    )(p, mu, w0, a0, wa_up, g_up, k_k, k_a, r_k, ln_g, ln_b, masks, csum, hsum)


def _alibi_slopes(n):
    return 2.0 ** (-8.0 * np.arange(1, n + 1) / n)


def _dil_bias(group):
    window, dilation = DIL_PATTERNS[group]
    n = window // dilation
    slopes = _alibi_slopes(DIL_HEADS).reshape(DIL_GROUPS, -1)[group]
    rel = n + np.arange(n)[:, None] - np.arange(2 * n)[None, :]
    valid = (rel >= 0) & (rel <= n)
    bias = -slopes[:, None, None] * (dilation * rel).astype(np.float64)[None]
    return np.where(valid[None], bias, NEG_INF).astype(np.float32)


def _dil_attend(blocks, bias_of, front):
    shape = blocks[0][0].shape
    heads = shape[1] // HEAD_DIM
    head = lax.broadcasted_iota(jnp.int32, shape, 1) // HEAD_DIM
    pairs = [(b, h) for b in range(len(blocks)) for h in range(heads)]
    s = [_dot_nt(jnp.where(head == h, blocks[b][0], jnp.zeros(shape, BF16)), blocks[b][1]) + bias_of(h)
         for b, h in pairs]
    if front is not None:
        s = [jnp.where(front, NEG_INF, x) for x in s]
    m = [jnp.max(x, axis=-1, keepdims=True) for x in s]
    e = [jnp.exp2(x - mx) for x, mx in zip(s, m)]
    l = [jnp.sum(x, axis=-1, keepdims=True) for x in e]
    per_half = 128 // HEAD_DIM
    pv = [_dot(x.astype(BF16), blocks[b][2][:, (h // per_half) * 128:(h // per_half + 1) * 128])
          for x, (b, h) in zip(e, pairs)]
    low = lax.broadcasted_iota(jnp.int32, (shape[0], 128), 1) < HEAD_DIM
    results = []
    for b in range(len(blocks)):
        outs, lses = [], []
        for half in range(heads // per_half):
            i0, i1 = b * heads + half * per_half, b * heads + half * per_half + 1
            outs.append(jnp.where(low, pv[i0] / l[i0], pv[i1] / l[i1]))
            lses.append(jnp.where(low, m[i0] + jnp.log(l[i0]) * _LOG2E, m[i1] + jnp.log(l[i1]) * _LOG2E))
        results.append((outs, lses))
    return results


def _dil_kernel(c0_ref, c1_ref, c2_ref, p0_ref, p1_ref, p2_ref, b0_ref, b1_ref, b2_ref, y_ref, og_ref, lg_ref):
    n = DIL_BLOCK
    W = BRANCH_WIDTH
    sub = DIL_SUB
    first = pl.program_id(1) == 0
    front = jnp.logical_and(first, lax.broadcasted_iota(jnp.int32, (n, 2 * n), 1) < n)
    qscale = HEAD_DIM ** -0.5 * _LOG2E
    kcols, vcols, qcols = slice(0, W), slice(W, 2 * W), slice(2 * W, 3 * W)

    def scaled(q):
        return (q.astype(F32) * qscale).astype(BF16)

    def put(g, rows, out, lse):
        for half in range(W // 128):
            og_ref[g, half, rows, :] = out[half]
            lg_ref[g, half, rows, :] = lse[half]

    def block_of(prev, cur):
        return (scaled(cur[:, qcols]), jnp.concatenate([prev[:, kcols], cur[:, kcols]], axis=0),
                jnp.concatenate([prev[:, vcols], cur[:, vcols]], axis=0))

    (out, lse), = _dil_attend([block_of(p0_ref[0], c0_ref[0, 0:n, :])], lambda h: b0_ref[h], front)
    put(0, pl.ds(0, n), out, lse)
    per0 = DIL_G0_BLOCKS_PER_TRIP

    def g0_body(i, carry):
        blocks, rows = [], []
        for j in range(per0):
            r0 = pl.multiple_of((i * per0 + j) * n, n)
            kv = c0_ref[0, pl.ds(r0, 2 * n), 0:2 * W]
            blocks.append((scaled(c0_ref[0, pl.ds(r0 + n, n), qcols]), kv[:, kcols], kv[:, vcols]))
            rows.append(pl.ds(r0 + n, n))
        for r, (out, lse) in zip(rows, _dil_attend(blocks, lambda h: b0_ref[h], None)):
            put(0, r, out, lse)
        return carry

    assert (DIL_TILE // n - 1) % per0 == 0
    lax.fori_loop(0, (DIL_TILE // n - 1) // per0, g0_body, 0)

    d1 = DIL_PATTERNS[1][1]

    def g1_rows(u, r):
        return pl.ds(pl.multiple_of(u * sub + r * n, n), n)

    def g1_run(u, prevs, front_mask):
        blocks = [block_of(prevs[r], c1_ref[0, g1_rows(u, r), :]) for r in range(d1)]
        for r, (out, lse) in enumerate(_dil_attend(blocks, lambda h: b1_ref[h], front_mask)):
            put(1, pl.ds(u * sub + r, n, stride=d1), out, lse)

    g1_run(0, [p1_ref[0, r * n:(r + 1) * n, :] for r in range(d1)], front)

    def g1_body(u, carry):
        g1_run(u, [c1_ref[0, g1_rows(u - 1, r), 0:2 * W] for r in range(d1)], None)
        return carry

    lax.fori_loop(1, DIL_TILE // sub, g1_body, 0)

    d2 = DIL_PATTERNS[2][1]
    piece = sub // d2
    per2 = DIL_G2_BLOCKS_PER_TRIP

    def g2_body(i, carry):
        def rows_of(ref, r, cols):
            return jnp.concatenate(
                [ref[0, pl.ds(pl.multiple_of(u * sub + r * piece, piece), piece), cols]
                 for u in range(DIL_TILE // sub)], axis=0)
        rs = [i * per2 + j for j in range(per2)]
        blocks = [block_of(rows_of(p2_ref, r, slice(0, 2 * W)), rows_of(c2_ref, r, slice(0, 3 * W))) for r in rs]
        for r, (out, lse) in zip(rs, _dil_attend(blocks, lambda h: b2_ref[h], front)):
            put(2, pl.ds(r, n, stride=d2), out, lse)
        return carry

    assert d2 % per2 == 0
    lax.fori_loop(0, d2 // per2, g2_body, 0)

    rows_per = 256

    def combine(j, carry):
        rows = pl.ds(pl.multiple_of(j * rows_per, rows_per), rows_per)
        for half in range(W // 128):
            ls = [lg_ref[g, half, rows, :] for g in range(DIL_GROUPS)]
            m = jnp.maximum(jnp.maximum(ls[0], ls[1]), ls[2])
            es = [jnp.exp2(x - m) for x in ls]
            num = es[0] * og_ref[0, half, rows, :] + es[1] * og_ref[1, half, rows, :] + es[2] * og_ref[2, half, rows, :]
            y_ref[0, rows, half * 128:(half + 1) * 128] = (num / (es[0] + es[1] + es[2])).astype(y_ref.dtype)
        return carry

    lax.fori_loop(0, DIL_TILE // rows_per, combine, 0)


def _dilated(pd0, pd1, pd2):
    B, S, _ = pd0.shape
    W = BRANCH_WIDTH
    T = DIL_TILE
    assert S % T == 0
    for (window, d) in DIL_PATTERNS:
        assert window // d == DIL_BLOCK
    assert DIL_PATTERNS[0][1] == 1 and DIL_PATTERNS[1][1] * DIL_BLOCK == DIL_SUB and DIL_PATTERNS[2][1] * DIL_BLOCK == T
    biases = [jnp.asarray(_dil_bias(g) * _LOG2E) for g in range(DIL_GROUPS)]
    cur = pl.BlockSpec((1, T, 3 * W), lambda b, i: (b, i, 0))
    prev = lambda rows: pl.BlockSpec((1, rows, 2 * W), lambda b, i: (b, jnp.maximum(i * (T // rows) - 1, 0), 0))
    return pl.pallas_call(
        _dil_kernel,
        grid=(B, S // T),
        in_specs=[cur, cur, cur, prev(DIL_BLOCK), prev(DIL_SUB), prev(T)] + [_const_spec(b.shape) for b in biases],
        out_specs=pl.BlockSpec((1, T, W), lambda b, i: (b, i, 0)),
        out_shape=jax.ShapeDtypeStruct((B, S, W), BF16),
        scratch_shapes=[pltpu.VMEM((DIL_GROUPS, W // 128, T, 128), F32)] * 2,
        compiler_params=_params("parallel", "arbitrary"),
        name="dilated",
    )(pd0, pd1, pd2, pd0, pd1, pd2, *biases)


def _diff_bias():
    t = DIFF_TILE
    slopes = _alibi_slopes(DIFF_HEADS) * _LOG2E
    rel = (np.arange(t)[None, :] - np.arange(t)[:, None]).astype(np.float64)
    off = -slopes[:, None, None] * rel[None]
    diag = np.where(rel[None] >= 0, off, NEG_INF)
    return off.astype(np.float32), diag.astype(np.float32)


def _diff_kernel(q_ref, k_ref, vt_ref, boff_ref, bdiag_ref, lamv_ref, gain_ref, hsum_ref, o_ref, m_ref, acc_ref,
                 s_ref, *, lambda_init):
    t = DIFF_TILE
    n_maps = 2 * DIFF_HEADS
    ones_rows = DIFF_ACC_ROWS - HEAD_DIM
    qi = pl.program_id(1)
    q = (q_ref[0].astype(F32) * (DIFF_QK_DIM ** -0.5 * _LOG2E)).astype(BF16)
    lane = lax.broadcasted_iota(jnp.int32, q.shape, 1)
    qs = [jnp.where(lane // DIFF_QK_DIM == c, q, jnp.zeros_like(q)) for c in range(n_maps)]
    slopes = _alibi_slopes(DIFF_HEADS) * _LOG2E

    m_ref[...] = jnp.full(m_ref.shape, NEG_INF, F32)
    acc_ref[...] = jnp.zeros_like(acc_ref)

    def keys(kb):
        return pl.ds(pl.multiple_of(kb * t, t), t)

    def fold(kb, slot, bias_ref, k_next):
        ones = jnp.ones((ones_rows, t), BF16)
        vt = [jnp.concatenate([vt_ref[h * HEAD_DIM:(h + 1) * HEAD_DIM, keys(kb)], ones], axis=0)
              for h in range(DIFF_HEADS)]
        shift = ((qi - kb) * t).astype(F32)
        for c in range(n_maps):
            if k_next is not None:
                s_ref[1 - slot, c] = _dot_nt(k_next, qs[c])
            block_bias = float(slopes[c // 2]) * shift
            s = s_ref[slot, c] + bias_ref[c // 2]
            m_prev = m_ref[c, 0:1, :]
            m_new = jnp.maximum(m_prev, jnp.max(s, axis=0, keepdims=True) - block_bias)
            e = jnp.exp2(s - (m_new + block_bias)).astype(BF16)
            m_ref[c] = jnp.broadcast_to(m_new, m_ref.shape[1:])
            acc_ref[c] = acc_ref[c] * jnp.exp2(m_prev - m_new) + _dot(vt[c // 2], e)

    k0 = k_ref[0, keys(0), :]
    for c in range(n_maps):
        s_ref[0, c] = _dot_nt(k0, qs[c])

    def pair(jj, carry):
        fold(2 * jj, 0, boff_ref, k_ref[0, keys(2 * jj + 1), :])
        fold(2 * jj + 1, 1, boff_ref, k_ref[0, keys(2 * jj + 2), :])
        return carry

    lax.fori_loop(0, qi // 2, pair, 0)

    @pl.when(qi % 2 == 1)
    def _():
        fold(qi - 1, 0, boff_ref, k_ref[0, keys(qi), :])
        fold(qi, 1, bdiag_ref, None)

    @pl.when(qi % 2 == 0)
    def _():
        fold(qi, 0, bdiag_ref, None)

    lamv = lamv_ref[...]
    lam = (jnp.exp(jnp.sum(lamv[0:1] * lamv[1:2], axis=-1, keepdims=True))
           - jnp.exp(jnp.sum(lamv[2:3] * lamv[3:4], axis=-1, keepdims=True)) + lambda_init)
    heads = []
    for h in range(DIFF_HEADS):
        a0 = acc_ref[2 * h]
        a1 = acc_ref[2 * h + 1]
        heads.append(a0[0:HEAD_DIM] / a0[HEAD_DIM:HEAD_DIM + 1]
                     - lam * (a1[0:HEAD_DIM] / a1[HEAD_DIM:HEAD_DIM + 1]))
    o = jnp.concatenate(heads, axis=0).T
    ms = _dot_split(o * o, hsum_ref[...]) * (1.0 / HEAD_DIM)
    o = o * lax.rsqrt(ms + DIFF_SUBLN_EPS) * gain_ref[...]
    o_ref[0] = (o * (1.0 - lambda_init)).astype(o_ref.dtype)


def _diff_attention(p_qk, p_vt, lamv, subln_g, lambda_init):
    B, S, _ = p_qk.shape
    t = DIFF_TILE
    W = BRANCH_WIDTH
    boff, bdiag = (jnp.asarray(b) for b in _diff_bias())
    hsum = jnp.asarray(_head_sum_mat(W, HEAD_DIM), BF16)
    gain = jnp.tile(subln_g.reshape(1, HEAD_DIM), (1, DIFF_HEADS))
    single = lambda shape: pl.BlockSpec(shape, lambda b, i: (0,) * len(shape), pipeline_mode=pl.Buffered(1))
    return pl.pallas_call(
        functools.partial(_diff_kernel, lambda_init=lambda_init),
        grid=(B, S // t),
        in_specs=[pl.BlockSpec((1, t, W), lambda b, i: (b, i, 0)), pl.BlockSpec((1, S, W), lambda b, i: (b, 0, 1)),
                  pl.BlockSpec((W, S), lambda b, i: (0, b)),
                  single(boff.shape), single(bdiag.shape), _const_spec(lamv.shape), _const_spec((1, W)),
                  _const_spec((W, W))],
        out_specs=pl.BlockSpec((1, t, W), lambda b, i: (b, i, 0)),
        out_shape=jax.ShapeDtypeStruct((B, S, W), BF16),
        scratch_shapes=[pltpu.VMEM((2 * DIFF_HEADS, 8, t), F32), pltpu.VMEM((2 * DIFF_HEADS, DIFF_ACC_ROWS, t), F32),
                        pltpu.VMEM((2, 2 * DIFF_HEADS, t, t), F32)],
        compiler_params=_params("parallel", "arbitrary"),
        name="diffattn",
    )(p_qk, p_qk, p_vt, boff, bdiag, lamv, gain, hsum)


def _conv_kernel(p_ref, w_ref, b_ref, g_ref, beta_ref, o_ref, u_ref):
    ts = p_ref.shape[1]
    Cc = CONV_CHANNELS
    halo = CONV_HALO
    rows = 64

    @pl.when(pl.program_id(1) == 0)
    def _():
        u_ref[0:halo, :] = jnp.zeros((halo, Cc), F32)

    @pl.when(pl.program_id(1) > 0)
    def _():
        u_ref[0:halo, :] = u_ref[ts:ts + halo, :]

    p = p_ref[0]
    u_ref[halo:halo + ts, :] = p[:, :Cc] * _sigmoid(p[:, Cc:])

    def block(i, carry):
        r0 = pl.multiple_of(i * rows, rows)
        acc = jnp.zeros((rows, Cc), F32) + b_ref[...]
        win = u_ref[pl.ds(r0, rows + halo), :]
        first_tap = halo - CONV_WIDTH + 1
        for b in range(8):
            wb = win if b == 0 else pltpu.roll(win, rows + halo - b, 0)
            for a in range(halo // 8 + 1):
                j = 8 * a + b - first_tap
                if 0 <= j < CONV_WIDTH:
                    acc = acc + w_ref[j:j + 1, :] * wb[8 * a:8 * a + rows]
        mu = jnp.mean(acc, axis=-1, keepdims=True)
        d = acc - mu
        var = jnp.mean(d * d, axis=-1, keepdims=True)
        y = d * lax.rsqrt(var + CONV_LN_EPS) * g_ref[...] + beta_ref[...]
        o_ref[0, pl.ds(r0, rows), :] = (y * _sigmoid(y)).astype(o_ref.dtype)
        return carry

    lax.fori_loop(0, ts // rows, block, 0)


def _conformer_conv(p_conv, dw_w, dw_b, ln_g, ln_b):
    B, S, _ = p_conv.shape
    ts = TOKEN_TILE
    Cc = CONV_CHANNELS
    vec = _const_spec((1, Cc))
    return pl.pallas_call(
        _conv_kernel,
        grid=(B, S // ts),
        in_specs=[pl.BlockSpec((1, ts, CONV_COLS), lambda b, s: (b, s, 0)), _const_spec((CONV_WIDTH, Cc)),
                  vec, vec, vec],
        out_specs=pl.BlockSpec((1, ts, Cc), lambda b, s: (b, s, 0)),
        out_shape=jax.ShapeDtypeStruct((B, S, Cc), BF16),
        scratch_shapes=[pltpu.VMEM((ts + CONV_HALO, Cc), F32)],
        compiler_params=_params("parallel", "arbitrary"),
        name="conformer_conv",
    )(p_conv, dw_w, dw_b, ln_g, ln_b)


def _merge_kernel(x_ref, ya_ref, yb_ref, yc_ref, yd_ref,
                  gpre_ref, gpost_ref, wg_ref, gb_ref, wb_ref, wo_ref, out_ref):
    x = x_ref[...]
    h = _rms(x, gpre_ref[...]).astype(BF16)
    ys = (ya_ref[...], yb_ref[...], yc_ref[...], yd_ref[...])
    merged = None
    for n in range(N_BRANCHES):
        gate = _sigmoid(_dot(h, wg_ref[n]) + gb_ref[n])
        term = gate * _dot(ys[n], wb_ref[n])
        merged = term if merged is None else merged + term
    y = _dot(merged.astype(BF16), wo_ref[...])
    out_ref[...] = x + _rms(y, gpost_ref[...])


def _merge(x2, ya, yb, yc, yd, gpre, gpost, w_gate, gate_bias, w_branch, w_out):
    T = x2.shape[0]
    tm = TOKEN_TILE
    D = D_MODEL
    W = BRANCH_WIDTH
    row = lambda w: pl.BlockSpec((tm, w), lambda i: (i, 0))
    single = lambda shape: pl.BlockSpec(shape, lambda i: (0,) * len(shape), pipeline_mode=pl.Buffered(1))
    return pl.pallas_call(
        _merge_kernel,
        grid=(T // tm,),
        in_specs=[row(D)] + [row(W)] * 4 + [_const_spec((1, D)), _const_spec((1, D)),
                  single((N_BRANCHES, D, D)), _const_spec((N_BRANCHES, 1, D)), single((N_BRANCHES, W, D)),
                  single((D, D))],
        out_specs=row(D),
        out_shape=jax.ShapeDtypeStruct((T, D), F32),
        compiler_params=_params("parallel"),
        name="merge",
    )(x2, ya, yb, yc, yd, gpre, gpost, w_gate, gate_bias, w_branch, w_out)


def _ffn_kernel(x_ref, gpre_ref, gpost_ref, wg_ref, wu_ref, wd_ref, out_ref):
    x = x_ref[...]
    h = _rms(x, gpre_ref[...]).astype(BF16)
    g = _dot(h, wg_ref[...])
    u = _dot(h, wu_ref[...])
    a = (g * _sigmoid(g) * u).astype(BF16)
    f = _dot(a, wd_ref[...])
    out_ref[...] = x + _rms(f, gpost_ref[...])


def _ffn(x2, gpre, gpost, w_gate, w_up, w_down):
    T = x2.shape[0]
    tm = TOKEN_TILE
    D = D_MODEL
    row = pl.BlockSpec((tm, D), lambda i: (i, 0))
    single = lambda shape: pl.BlockSpec(shape, lambda i: (0,) * len(shape), pipeline_mode=pl.Buffered(1))
    return pl.pallas_call(
        _ffn_kernel,
        grid=(T // tm,),
        in_specs=[row, _const_spec((1, D)), _const_spec((1, D)), single((D, D_FF)), single((D, D_FF)),
                  single((D_FF, D))],
        out_specs=row,
        out_shape=jax.ShapeDtypeStruct((T, D), F32),
        compiler_params=_params("parallel"),
        name="swiglu",
    )(x2, gpre, gpost, w_gate, w_up, w_down)


def _layer(x2, B, S, l, prm):
    D = D_MODEL
    W = BRANCH_WIDTH
    row = lambda a: a.reshape(1, -1)
    w_in = prm["w_in"][l]
    w_dil = w_in[:, RWKV_COLS:RWKV_COLS + DIL_COLS].reshape(D, 3, DIL_GROUPS, W)
    w_dil = jnp.concatenate([w_dil[:, 1], w_dil[:, 2], w_dil[:, 0]], axis=-1).transpose(1, 0, 2)
    diff0 = RWKV_COLS + DIL_COLS
    w_mix = jnp.concatenate([w_in[:, :RWKV_COLS]] + [w_dil[g] for g in range(DIL_GROUPS)]
                            + [w_in[:, diff0:diff0 + DIFF_QK_COLS], w_in[:, diff0 + DIFF_COLS:MIX_COLS]],
                            axis=1).astype(BF16)
    w_vt = w_in[:, diff0 + DIFF_QK_COLS:diff0 + DIFF_COLS].T.astype(BF16)
    w_gate = w_in[:, MIX_COLS:].reshape(D, N_BRANCHES, D).transpose(1, 0, 2).astype(BF16)

    p_rwkv, pd0, pd1, pd2, p_qk, p_conv, p_vt = _inproj(x2, row(prm["norm_mix_pre"][l]), w_mix, w_vt)

    zeros = jnp.zeros((64, W), F32)
    wa_up = jnp.concatenate([jnp.concatenate([prm["rwkv_w_up"][l], zeros], axis=1),
                             jnp.concatenate([zeros, prm["rwkv_a_up"][l]], axis=1)], axis=0).astype(BF16)
    ya = _rwkv(p_rwkv.reshape(B, S, RWKV_COLS), row(prm["rwkv_mu"][l]), row(prm["rwkv_w0"][l]),
               row(prm["rwkv_a0"][l]), wa_up, prm["rwkv_g_up"][l].astype(BF16), row(prm["rwkv_k_k"][l]),
               row(prm["rwkv_k_a"][l]), row(prm["rwkv_r_k"][l]), row(prm["rwkv_ln_g"][l]),
               row(prm["rwkv_ln_b"][l]))

    group_w = DIL_COLS // DIL_GROUPS
    yb = _dilated(pd0.reshape(B, S, group_w), pd1.reshape(B, S, group_w), pd2.reshape(B, S, group_w))

    lambda_init = 0.8 - 0.6 * math.exp(-0.3 * l)
    lamv = jnp.stack([prm["diff_lam_q1"][l], prm["diff_lam_k1"][l], prm["diff_lam_q2"][l], prm["diff_lam_k2"][l]])
    yc = _diff_attention(p_qk.reshape(B, S, DIFF_QK_COLS), p_vt, lamv, prm["diff_subln_g"][l], lambda_init)

    yd = _conformer_conv(p_conv.reshape(B, S, CONV_COLS), prm["conv_dw_w"][l].reshape(CONV_WIDTH, CONV_CHANNELS),
                         row(prm["conv_dw_b"][l]), row(prm["conv_ln_g"][l]), row(prm["conv_ln_b"][l]))

    x2 = _merge(x2, ya.reshape(B * S, W), yb.reshape(B * S, W), yc.reshape(B * S, W), yd.reshape(B * S, W),
                row(prm["norm_mix_pre"][l]), row(prm["norm_mix_post"][l]), w_gate,
                prm["gate_bias"][l].reshape(N_BRANCHES, 1, D), prm["w_branch"][l].astype(BF16),
                prm["w_out"][l].astype(BF16))

    return _ffn(x2, row(prm["norm_ffn_pre"][l]), row(prm["norm_ffn_post"][l]), prm["ffn_w_gate"][l].astype(BF16),
                prm["ffn_w_up"][l].astype(BF16), prm["ffn_w_down"][l].astype(BF16))


def kernel(x, norm_mix_pre, norm_mix_post, norm_ffn_pre, norm_ffn_post, w_in, gate_bias, rwkv_mu, rwkv_w0, rwkv_w_up, rwkv_a0, rwkv_a_up, rwkv_g_up, rwkv_k_k, rwkv_k_a, rwkv_r_k, rwkv_ln_g, rwkv_ln_b, diff_lam_q1, diff_lam_k1, diff_lam_q2, diff_lam_k2, diff_subln_g, conv_dw_w, conv_dw_b, conv_ln_g, conv_ln_b, w_branch, w_out, ffn_w_gate, ffn_w_up, ffn_w_down):
    prm = dict(locals())
    B, S, D = x.shape
    x2 = x.reshape(B * S, D)
    for l in range(w_in.shape[0]):
        x2 = _layer(x2, B, S, l, prm)
    return x2.reshape(B, S, D)
```

```python
import functools
import math

import numpy as np
import jax
import jax.numpy as jnp
from jax import lax
from jax.experimental import pallas as pl
from jax.experimental.pallas import tpu as pltpu

F32 = jnp.float32
BF16 = jnp.bfloat16

D_MODEL = 1024
DEPTH = 4
HEAD_DIM = 64
BRANCH_WIDTH = 256
N_BRANCHES = 4

RWKV_HEADS = 4
RWKV_WIDTH = 256
RWKV_COLS = 1024
RWKV_DECAY_SCALE = 0.606531
RWKV_LN_EPS = 64e-5
RWKV_CHUNK = 64
RWKV_TILE = 256
RWKV_SEQS = 2

DIL_PATTERNS = ((128, 1), (512, 4), (2048, 16))
DIL_GROUPS = 3
DIL_HEADS = 12
DIL_QKV = 768
DIL_COLS = 2304
DIL_BLOCK = 128
DIL_TILE = 2048
DIL_SUB = 512
DIL_G0_BLOCKS_PER_TRIP = 3
DIL_G2_BLOCKS_PER_TRIP = 4

DIFF_HEADS = 4
DIFF_QK_DIM = 32
DIFF_COLS = 768
DIFF_TILE = 512
DIFF_QK_COLS = 512
DIFF_ACC_ROWS = HEAD_DIM + 16
DIFF_SUBLN_EPS = 1e-5

CONV_CHANNELS = 256
CONV_WIDTH = 31
CONV_COLS = 512
CONV_LN_EPS = 1e-5
CONV_HALO = 32
CONV_ROWS = 64

MIX_COLS = RWKV_COLS + DIL_COLS + DIFF_COLS + CONV_COLS
D_FF = 2816
NORM_EPS = 1e-6
NEG_INF = -1e30
_LOG2E = 1.4426950408889634

VMEM_LIMIT_BYTES = 56 * 1024 * 1024
TOKEN_TILE = 512
MXU_COLS = 256


def _dot(a, b):
    return jnp.dot(a, b, preferred_element_type=F32)


def _dot_nt(a, b):
    return lax.dot_general(a, b, (((1,), (1,)), ((), ())), preferred_element_type=F32)


def _dot_tn(a, b):
    return lax.dot_general(a, b, (((0,), (0,)), ((), ())), preferred_element_type=F32)


def _dot_split(x, m):
    hi = x.astype(BF16)
    lo = (x - hi.astype(F32)).astype(BF16)
    return _dot(hi, m) + _dot(lo, m)


def _dot_split_rhs(m, x):
    hi = x.astype(BF16)
    lo = (x - hi.astype(F32)).astype(BF16)
    return _dot(m, hi) + _dot(m, lo)


def _rms(x, gain):
    return x * lax.rsqrt(jnp.mean(x * x, axis=-1, keepdims=True) + NORM_EPS) * gain


def _sigmoid(x):
    return 1.0 / (1.0 + jnp.exp(-x))


def _const_spec(shape):
    nd = len(shape)
    return pl.BlockSpec(shape, lambda *_: (0,) * nd)


def _params(*sem):
    return pltpu.CompilerParams(dimension_semantics=sem, vmem_limit_bytes=VMEM_LIMIT_BYTES)


def _inproj_kernel(x_ref, g_ref, w_ref, wvt_ref, cw_ref, cb_ref, cg_ref, cbeta_ref,
                   o_rwkv, o_dil0, o_dil1, o_dil2, o_diff, o_conv, o_vt, h_ref, u_ref, *, tiles_per_seq):
    tm = x_ref.shape[0]
    hf = _rms(x_ref[...], g_ref[...])
    h = hf.astype(BF16)
    conv0 = w_ref.shape[1] - CONV_COLS
    p_conv = jnp.concatenate([_dot(h, w_ref[:, conv0 + c0:conv0 + c0 + MXU_COLS])
                              for c0 in range(0, CONV_COLS, MXU_COLS)], axis=1)
    _conformer_tile(p_conv, pl.program_id(0) % tiles_per_seq == 0, u_ref, cw_ref, cb_ref, cg_ref, cbeta_ref, o_conv)
    slabs = D_MODEL // 128
    for j in range(slabs):
        h_ref[j] = hf[:, j * 128:(j + 1) * 128]

    def permuted(d):
        cols = [jnp.concatenate([h_ref[j, pl.ds(r, tm // d, stride=d), :] for r in range(d)], axis=0)
                for j in range(slabs)]
        return jnp.concatenate(cols, axis=1).astype(BF16)

    group_w = DIL_COLS // DIL_GROUPS
    col = 0
    for o_ref, width, lhs in ((o_rwkv, RWKV_COLS, h), (o_dil0, group_w, h),
                              (o_dil1, group_w, permuted(DIL_PATTERNS[1][1])),
                              (o_dil2, group_w, permuted(DIL_PATTERNS[2][1])),
                              (o_diff, DIFF_QK_COLS, h)):
        for c0 in range(0, width, MXU_COLS):
            y = _dot(lhs, w_ref[:, col + c0:col + c0 + MXU_COLS])
            o_ref[:, c0:c0 + MXU_COLS] = y.astype(o_ref.dtype)
        col += width
    o_vt[...] = _dot_nt(wvt_ref[...], h).astype(o_vt.dtype)


def _inproj(x2, seq_len, gain, w_mix, w_vt, dw_w, dw_b, ln_g, ln_b):
    T = x2.shape[0]
    tm = TOKEN_TILE
    assert tm == DIL_SUB and seq_len % tm == 0
    group_w = DIL_COLS // DIL_GROUPS
    W = BRANCH_WIDTH
    Cc = CONV_CHANNELS
    row = lambda w: pl.BlockSpec((tm, w), lambda i: (i, 0))
    bf = lambda w: jax.ShapeDtypeStruct((T, w), BF16)
    vec = _const_spec((1, Cc))
    return pl.pallas_call(
        functools.partial(_inproj_kernel, tiles_per_seq=seq_len // tm),
        grid=(T // tm,),
        in_specs=[row(D_MODEL), _const_spec((1, D_MODEL)), _const_spec((D_MODEL, MIX_COLS - W)),
                  _const_spec((W, D_MODEL)), _const_spec((CONV_WIDTH, Cc)), vec, vec, vec],
        out_specs=[row(RWKV_COLS), row(group_w), row(group_w), row(group_w), row(DIFF_QK_COLS), row(Cc),
                   pl.BlockSpec((W, tm), lambda i: (0, i))],
        out_shape=[jax.ShapeDtypeStruct((T, RWKV_COLS), F32), bf(group_w), bf(group_w), bf(group_w),
                   bf(DIFF_QK_COLS), bf(Cc), jax.ShapeDtypeStruct((W, T), BF16)],
        scratch_shapes=[pltpu.VMEM((D_MODEL // 128, tm, 128), F32), pltpu.VMEM((tm + CONV_HALO, Cc), F32)],
        compiler_params=_params("arbitrary"),
        name="inproj",
    )(x2, gain, w_mix, w_vt, dw_w, dw_b, ln_g, ln_b)


_MASK_STRICT, _MASK_INCL, _MASK_EYE, _MASK_LEVEL0 = 0, 1, 2, 3
_INV_LEVELS = (1, 2, 4, 8, 16, 32)


def _rwkv_masks():
    n = RWKV_HEADS * RWKV_CHUNK
    r = np.arange(n)[:, None]
    c = np.arange(n)[None, :]
    same = (r // RWKV_CHUNK) == (c // RWKV_CHUNK)
    out = [same & (r > c), same & (r >= c), r == c]
    for s in _INV_LEVELS:
        out.append((r // (2 * s) == c // (2 * s)) & (r % (2 * s) >= s) & (c % (2 * s) < s))
    head = (r // RWKV_CHUNK) == (c // HEAD_DIM)
    out.append(head)
    return np.stack(out).astype(np.float32)


_MASK_HEAD = 3 + len(_INV_LEVELS)


def _chunk_sum_mats(ts):
    r = np.arange(ts)[:, None]
    c = np.arange(ts)[None, :]
    same = (r // RWKV_CHUNK) == (c // RWKV_CHUNK)
    return np.stack([same & (c <= r), same]).astype(np.float32)


def _head_sum_mat(width, group):
    r = np.arange(width)[:, None]
    c = np.arange(width)[None, :]
    return ((r // group) == (c // group)).astype(np.float32)


def _rwkv_kernel(p_ref, mu_ref, w0_ref, a0_ref, wa_ref, gup_ref, kk_ref, ka_ref, rk_ref, lng_ref, lnb_ref,
                 masks_ref, csum_ref, hsum_ref, o_ref,
                 state_ref, prev_ref, at_ref, bt_ref, kt_ref, rt_ref, bh_ref, kh_ref, v_ref, ec_ref, oc_ref,
                 lhs_ref, upd_ref, vr_ref, arb_ref, ut_ref, arkv_ref):
    nseq, ts = p_ref.shape[0], p_ref.shape[1]
    W = RWKV_WIDTH
    C = RWKV_CHUNK
    n = RWKV_HEADS * C
    nc = ts // C

    @pl.when(pl.program_id(1) == 0)
    def _():
        state_ref[...] = jnp.zeros_like(state_ref)
        prev_ref[...] = jnp.zeros_like(prev_ref)

    hsum = hsum_ref[...]
    kept = []
    for q in range(nseq):
        p = p_ref[q]
        row = lax.broadcasted_iota(jnp.int32, p.shape, 0)
        p_prev = jnp.where(row == 0, prev_ref[q], pltpu.roll(p, 1, 0))
        prev_ref[q] = p[ts - 1:ts, :]
        p = p + (p_prev - p) * mu_ref[...]

        r = p[:, 0:W]
        k = p[:, W:2 * W]
        v = p[:, 2 * W:3 * W]
        zwa = p[:, 3 * W:3 * W + 128]
        zg = p[:, 3 * W + 128:]
        lane = lax.broadcasted_iota(jnp.int32, zwa.shape, 1)
        tz = jnp.where(lane < 64, jnp.tanh(zwa), zwa)
        wa = _dot(tz.astype(BF16), wa_ref[...])
        logw = -RWKV_DECAY_SCALE * _sigmoid(w0_ref[...] + wa[:, :W])
        alpha = _sigmoid(a0_ref[...] + wa[:, W:])
        g = _dot(_sigmoid(zg).astype(BF16), gup_ref[...])

        kk = k * kk_ref[...]
        kk = kk * lax.rsqrt(jnp.maximum(_dot_split(kk * kk, hsum), 1e-24))
        k2 = k * (1.0 + (alpha - 1.0) * ka_ref[...])
        bvec = kk * alpha

        cum = _dot_split_rhs(csum_ref[0], logw)
        cum_end = _dot_split_rhs(csum_ref[1], logw)
        e_neg = jnp.exp(-cum)
        e_rem = jnp.exp(cum_end - cum)
        at_ref[q] = -kk * jnp.exp(cum - logw)
        bt_ref[q] = bvec * e_neg
        kt_ref[q] = k2 * e_neg
        rt_ref[q] = r * jnp.exp(cum)
        bh_ref[q] = bvec * e_rem
        kh_ref[q] = k2 * e_rem
        v_ref[q] = v
        ec_ref[q] = jnp.exp(cum_end)
        kept.append((r * k2 * rk_ref[...], v, g))

    def tile4(x):
        x4 = jnp.concatenate([x, x, x, x], axis=0)
        return jnp.where(masks_ref[_MASK_HEAD] > 0, x4, 0.0).astype(BF16)

    def keep(m, x):
        return jnp.where(masks_ref[m] > 0, x, 0.0)

    chains = [(q, c) for q in range(nseq) for c in range(nc)]
    rows = [slice(c * C, (c + 1) * C) for _, c in chains]
    at = [tile4(at_ref[q, r, :]) for (q, _), r in zip(chains, rows)]
    vr = [tile4(v_ref[q, r, :]) for (q, _), r in zip(chains, rows)]
    a_ab, a_ak = [], []
    for j, ((q, c), r) in enumerate(zip(chains, rows)):
        rt = tile4(rt_ref[q, r, :])
        lhs = jnp.concatenate([at[j], rt], axis=0)
        rhs = jnp.concatenate([tile4(bt_ref[q, r, :]), tile4(kt_ref[q, r, :])], axis=0)
        a = _dot_nt(lhs, rhs)
        a_ab.append(keep(_MASK_STRICT, a[:n, :n]))
        a_ak.append(keep(_MASK_STRICT, a[:n, n:]).astype(BF16))
        arb_ref[q, c] = keep(_MASK_INCL, a[n:, :n]).astype(BF16)
        arkv_ref[q, c] = _dot(keep(_MASK_INCL, a[n:, n:]).astype(BF16), vr[j])
        lhs_ref[q, c, n:2 * n, :] = rt
        vr_ref[q, c] = vr[j]
        upd_ref[q, c, 0:n, :] = tile4(bh_ref[q, r, :])
        upd_ref[q, c, n:2 * n, :] = tile4(kh_ref[q, r, :])
    akv = [_dot(a_ak[j], vr[j]).astype(BF16) for j in range(len(chains))]
    t = [masks_ref[_MASK_EYE] + keep(_MASK_LEVEL0, a) for a in a_ab]
    for li in range(1, len(_INV_LEVELS)):
        tb = [x.astype(BF16) for x in t]
        xt = [_dot(keep(_MASK_LEVEL0 + li, a).astype(BF16), y).astype(BF16) for a, y in zip(a_ab, tb)]
        t = [x + _dot(y, z) for x, y, z in zip(t, tb, xt)]
    for j, (q, c) in enumerate(chains):
        tb = t[j].astype(BF16)
        lhs_ref[q, c, 0:n, :] = _dot(tb, at[j]).astype(BF16)
        ut_ref[q, c] = _dot(tb, akv[j])

    for c in range(nc):
        s = [state_ref[q] for q in range(nseq)]
        g_us = [_dot_nt(lhs_ref[q, c], s[q].astype(BF16)) for q in range(nseq)]
        ub = [(g_us[q][:n] + ut_ref[q, c]).astype(BF16) for q in range(nseq)]
        o = [g_us[q][n:] + _dot(arb_ref[q, c], ub[q]) + arkv_ref[q, c] for q in range(nseq)]
        upd = [_dot_tn(jnp.concatenate([ub[q], vr_ref[q, c]], axis=0), upd_ref[q, c]) for q in range(nseq)]
        for q in range(nseq):
            oc_ref[q, c * C:(c + 1) * C, :] = o[q][0:C] + o[q][C:2 * C] + o[q][2 * C:3 * C] + o[q][3 * C:4 * C]
            state_ref[q] = s[q] * ec_ref[q, c * C:c * C + 1, :] + upd[q]

    hmean = hsum * (1.0 / HEAD_DIM)
    for q, (rk2, v, g) in enumerate(kept):
        o = oc_ref[q]
        mean = _dot_split(o, hmean)
        d = o - mean
        var = _dot_split(d * d, hmean)
        o = d * lax.rsqrt(var + RWKV_LN_EPS) * lng_ref[...] + lnb_ref[...]
        bonus = _dot_split(rk2, hsum)
        o_ref[q] = ((o + bonus * v) * g).astype(o_ref.dtype)


def _rwkv(p, mu, w0, a0, wa_up, g_up, k_k, k_a, r_k, ln_g, ln_b):
    B, S, _ = p.shape
    ts = RWKV_TILE
    nseq = RWKV_SEQS
    assert B % nseq == 0 and S % ts == 0
    W = RWKV_WIDTH
    n = RWKV_HEADS * RWKV_CHUNK
    nc = ts // RWKV_CHUNK
    masks = jnp.asarray(_rwkv_masks())
    csum = jnp.asarray(_chunk_sum_mats(ts), BF16)
    hsum = jnp.asarray(_head_sum_mat(W, HEAD_DIM), BF16)
    vec = _const_spec((1, W))
    buf = lambda: pltpu.VMEM((nseq, ts, W), F32)
    return pl.pallas_call(
        _rwkv_kernel,
        grid=(B // nseq, S // ts),
        in_specs=[pl.BlockSpec((nseq, ts, RWKV_COLS), lambda b, s: (b, s, 0)),
                  _const_spec((1, RWKV_COLS)), vec, vec, _const_spec((128, 2 * W)), _const_spec((128, W)),
                  vec, vec, vec, vec, vec,
                  _const_spec(masks.shape), _const_spec(csum.shape), _const_spec(hsum.shape)],
        out_specs=pl.BlockSpec((nseq, ts, W), lambda b, s: (b, s, 0)),
        out_shape=jax.ShapeDtypeStruct((B, S, W), BF16),
        scratch_shapes=[pltpu.VMEM((nseq, n, n), F32), pltpu.VMEM((nseq, 1, RWKV_COLS), F32)]
                       + [buf() for _ in range(9)]
                       + [pltpu.VMEM((nseq, nc, 2 * n, n), BF16), pltpu.VMEM((nseq, nc, 2 * n, n), BF16),
                          pltpu.VMEM((nseq, nc, n, n), BF16), pltpu.VMEM((nseq, nc, n, n), BF16),
                          pltpu.VMEM((nseq, nc, n, n), F32), pltpu.VMEM((nseq, nc, n, n), F32)],
        compiler_params=_params("parallel", "arbitrary"),
        name="rwkv7",
    )(p, mu, w0, a0, wa_up, g_up, k_k, k_a, r_k, ln_g, ln_b, masks, csum, hsum)


def _alibi_slopes(n):
    return 2.0 ** (-8.0 * np.arange(1, n + 1) / n)


def _dil_bias(group):
    window, dilation = DIL_PATTERNS[group]
    n = window // dilation
    slopes = _alibi_slopes(DIL_HEADS).reshape(DIL_GROUPS, -1)[group]
    rel = n + np.arange(n)[:, None] - np.arange(2 * n)[None, :]
    valid = (rel >= 0) & (rel <= n)
    bias = -slopes[:, None, None] * (dilation * rel).astype(np.float64)[None]
    return np.where(valid[None], bias, NEG_INF).astype(np.float32)


def _dil_attend(blocks, bias_of, front):
    shape = blocks[0][0].shape
    heads = shape[1] // HEAD_DIM
    head = lax.broadcasted_iota(jnp.int32, shape, 1) // HEAD_DIM
    pairs = [(b, h) for b in range(len(blocks)) for h in range(heads)]
    s = [_dot_nt(jnp.where(head == h, blocks[b][0], jnp.zeros(shape, BF16)), blocks[b][1]) + bias_of(h)
         for b, h in pairs]
    if front is not None:
        s = [jnp.where(front, NEG_INF, x) for x in s]
    m = [jnp.max(x, axis=-1, keepdims=True) for x in s]
    e = [jnp.exp2(x - mx) for x, mx in zip(s, m)]
    l = [jnp.sum(x, axis=-1, keepdims=True) for x in e]
    per_half = 128 // HEAD_DIM
    pv = [_dot(x.astype(BF16), blocks[b][2][:, (h // per_half) * 128:(h // per_half + 1) * 128])
          for x, (b, h) in zip(e, pairs)]
    low = lax.broadcasted_iota(jnp.int32, (shape[0], 128), 1) < HEAD_DIM
    results = []
    for b in range(len(blocks)):
        outs, lses = [], []
        for half in range(heads // per_half):
            i0, i1 = b * heads + half * per_half, b * heads + half * per_half + 1
            outs.append(jnp.where(low, pv[i0] / l[i0], pv[i1] / l[i1]))
            lses.append(jnp.where(low, m[i0] + jnp.log(l[i0]) * _LOG2E, m[i1] + jnp.log(l[i1]) * _LOG2E))
        results.append((outs, lses))
    return results


def _dil_kernel(c0_ref, c1_ref, c2_ref, p0_ref, p1_ref, p2_ref, b0_ref, b1_ref, b2_ref, y_ref, og_ref, lg_ref):
    n = DIL_BLOCK
    W = BRANCH_WIDTH
    sub = DIL_SUB
    first = pl.program_id(1) == 0
    front = jnp.logical_and(first, lax.broadcasted_iota(jnp.int32, (n, 2 * n), 1) < n)
    qscale = HEAD_DIM ** -0.5 * _LOG2E
    kcols, vcols, qcols = slice(0, W), slice(W, 2 * W), slice(2 * W, 3 * W)

    def scaled(q):
        return (q.astype(F32) * qscale).astype(BF16)

    def put(g, rows, out, lse):
        for half in range(W // 128):
            og_ref[g, half, rows, :] = out[half]
            lg_ref[g, half, rows, :] = lse[half]

    def block_of(prev, cur):
        return (scaled(cur[:, qcols]), jnp.concatenate([prev[:, kcols], cur[:, kcols]], axis=0),
                jnp.concatenate([prev[:, vcols], cur[:, vcols]], axis=0))

    (out, lse), = _dil_attend([block_of(p0_ref[0], c0_ref[0, 0:n, :])], lambda h: b0_ref[h], front)
    put(0, pl.ds(0, n), out, lse)
    per0 = DIL_G0_BLOCKS_PER_TRIP

    def g0_body(i, carry):
        blocks, rows = [], []
        for j in range(per0):
            r0 = pl.multiple_of((i * per0 + j) * n, n)
            kv = c0_ref[0, pl.ds(r0, 2 * n), 0:2 * W]
            blocks.append((scaled(c0_ref[0, pl.ds(r0 + n, n), qcols]), kv[:, kcols], kv[:, vcols]))
            rows.append(pl.ds(r0 + n, n))
        for r, (out, lse) in zip(rows, _dil_attend(blocks, lambda h: b0_ref[h], None)):
            put(0, r, out, lse)
        return carry

    assert (DIL_TILE // n - 1) % per0 == 0
    lax.fori_loop(0, (DIL_TILE // n - 1) // per0, g0_body, 0)

    d1 = DIL_PATTERNS[1][1]

    def g1_rows(u, r):
        return pl.ds(pl.multiple_of(u * sub + r * n, n), n)

    def g1_run(u, prevs, front_mask):
        blocks = [block_of(prevs[r], c1_ref[0, g1_rows(u, r), :]) for r in range(d1)]
        for r, (out, lse) in enumerate(_dil_attend(blocks, lambda h: b1_ref[h], front_mask)):
            put(1, pl.ds(u * sub + r, n, stride=d1), out, lse)

    g1_run(0, [p1_ref[0, r * n:(r + 1) * n, :] for r in range(d1)], front)

    def g1_body(u, carry):
        g1_run(u, [c1_ref[0, g1_rows(u - 1, r), 0:2 * W] for r in range(d1)], None)
        return carry

    lax.fori_loop(1, DIL_TILE // sub, g1_body, 0)

    d2 = DIL_PATTERNS[2][1]
    piece = sub // d2
    per2 = DIL_G2_BLOCKS_PER_TRIP

    def g2_body(i, carry):
        def rows_of(ref, r, cols):
            return jnp.concatenate(
                [ref[0, pl.ds(pl.multiple_of(u * sub + r * piece, piece), piece), cols]
                 for u in range(DIL_TILE // sub)], axis=0)
        rs = [i * per2 + j for j in range(per2)]
        blocks = [block_of(rows_of(p2_ref, r, slice(0, 2 * W)), rows_of(c2_ref, r, slice(0, 3 * W))) for r in rs]
        for r, (out, lse) in zip(rs, _dil_attend(blocks, lambda h: b2_ref[h], front)):
            put(2, pl.ds(r, n, stride=d2), out, lse)
        return carry

    assert d2 % per2 == 0
    lax.fori_loop(0, d2 // per2, g2_body, 0)

    rows_per = 256

    def combine(j, carry):
        rows = pl.ds(pl.multiple_of(j * rows_per, rows_per), rows_per)
        for half in range(W // 128):
            ls = [lg_ref[g, half, rows, :] for g in range(DIL_GROUPS)]
            m = jnp.maximum(jnp.maximum(ls[0], ls[1]), ls[2])
            es = [jnp.exp2(x - m) for x in ls]
            num = es[0] * og_ref[0, half, rows, :] + es[1] * og_ref[1, half, rows, :] + es[2] * og_ref[2, half, rows, :]
            y_ref[0, rows, half * 128:(half + 1) * 128] = (num / (es[0] + es[1] + es[2])).astype(y_ref.dtype)
        return carry

    lax.fori_loop(0, DIL_TILE // rows_per, combine, 0)


def _dilated(pd0, pd1, pd2):
    B, S, _ = pd0.shape
    W = BRANCH_WIDTH
    T = DIL_TILE
    assert S % T == 0
    for (window, d) in DIL_PATTERNS:
        assert window // d == DIL_BLOCK
    assert DIL_PATTERNS[0][1] == 1 and DIL_PATTERNS[1][1] * DIL_BLOCK == DIL_SUB and DIL_PATTERNS[2][1] * DIL_BLOCK == T
    biases = [jnp.asarray(_dil_bias(g) * _LOG2E) for g in range(DIL_GROUPS)]
    cur = pl.BlockSpec((1, T, 3 * W), lambda b, i: (b, i, 0))
    prev = lambda rows: pl.BlockSpec((1, rows, 2 * W), lambda b, i: (b, jnp.maximum(i * (T // rows) - 1, 0), 0))
    return pl.pallas_call(
        _dil_kernel,
        grid=(B, S // T),
        in_specs=[cur, cur, cur, prev(DIL_BLOCK), prev(DIL_SUB), prev(T)] + [_const_spec(b.shape) for b in biases],
        out_specs=pl.BlockSpec((1, T, W), lambda b, i: (b, i, 0)),
        out_shape=jax.ShapeDtypeStruct((B, S, W), BF16),
        scratch_shapes=[pltpu.VMEM((DIL_GROUPS, W // 128, T, 128), F32)] * 2,
        compiler_params=_params("parallel", "arbitrary"),
        name="dilated",
    )(pd0, pd1, pd2, pd0, pd1, pd2, *biases)


def _diff_bias():
    t = DIFF_TILE
    slopes = _alibi_slopes(DIFF_HEADS) * _LOG2E
    rel = (np.arange(t)[None, :] - np.arange(t)[:, None]).astype(np.float64)
    off = -slopes[:, None, None] * rel[None]
    diag = np.where(rel[None] >= 0, off, NEG_INF)
    return off.astype(np.float32), diag.astype(np.float32)


def _diff_kernel(q_ref, k_ref, vt_ref, boff_ref, bdiag_ref, lamv_ref, gain_ref, hsum_ref, o_ref, m_ref, acc_ref,
                 s_ref, qt_ref, *, lambda_init):
    t = DIFF_TILE
    n_maps = 2 * DIFF_HEADS
    ones_rows = DIFF_ACC_ROWS - HEAD_DIM
    qi = pl.program_id(1)
    q = q_ref[0].astype(F32) * (DIFF_QK_DIM ** -0.5 * _LOG2E)
    maps_per_half = 128 // DIFF_QK_DIM
    lane = lax.broadcasted_iota(jnp.int32, (t, 128), 1)
    for c in range(n_maps):
        half = c // maps_per_half
        qh = jnp.where(lane // DIFF_QK_DIM == c % maps_per_half, q[:, half * 128:(half + 1) * 128], 0.0)
        qt_ref[c] = qh.T.astype(BF16)
    slopes = _alibi_slopes(DIFF_HEADS) * _LOG2E

    def scores(k, c):
        half = c // maps_per_half
        return _dot(k[:, half * 128:(half + 1) * 128], qt_ref[c])

    m_ref[...] = jnp.full(m_ref.shape, NEG_INF, F32)
    acc_ref[...] = jnp.zeros_like(acc_ref)

    def keys(kb):
        return pl.ds(pl.multiple_of(kb * t, t), t)

    def fold(kb, slot, bias_ref, k_next):
        ones = jnp.ones((ones_rows, t), BF16)
        vt = [jnp.concatenate([vt_ref[h * HEAD_DIM:(h + 1) * HEAD_DIM, keys(kb)], ones], axis=0)
              for h in range(DIFF_HEADS)]
        shift = ((qi - kb) * t).astype(F32)
        for c in range(n_maps):
            if k_next is not None:
                s_ref[1 - slot, c] = scores(k_next, c)
            block_bias = float(slopes[c // 2]) * shift
            s = s_ref[slot, c] + bias_ref[c // 2]
            m_prev = m_ref[c, 0:1, :]
            m_new = jnp.maximum(m_prev, jnp.max(s, axis=0, keepdims=True) - block_bias)
            e = jnp.exp2(s - (m_new + block_bias)).astype(BF16)
            m_ref[c] = jnp.broadcast_to(m_new, m_ref.shape[1:])
            acc_ref[c] = acc_ref[c] * jnp.exp2(m_prev - m_new) + _dot(vt[c // 2], e)

    k0 = k_ref[0, keys(0), :]
    for c in range(n_maps):
        s_ref[0, c] = scores(k0, c)

    def pair(jj, carry):
        fold(2 * jj, 0, boff_ref, k_ref[0, keys(2 * jj + 1), :])
        fold(2 * jj + 1, 1, boff_ref, k_ref[0, keys(2 * jj + 2), :])
        return carry

    lax.fori_loop(0, qi // 2, pair, 0)

    @pl.when(qi % 2 == 1)
    def _():
        fold(qi - 1, 0, boff_ref, k_ref[0, keys(qi), :])
        fold(qi, 1, bdiag_ref, None)

    @pl.when(qi % 2 == 0)
    def _():
        fold(qi, 0, bdiag_ref, None)

    lamv = lamv_ref[...]
    lam = (jnp.exp(jnp.sum(lamv[0:1] * lamv[1:2], axis=-1, keepdims=True))
           - jnp.exp(jnp.sum(lamv[2:3] * lamv[3:4], axis=-1, keepdims=True)) + lambda_init)
    heads = []
    for h in range(DIFF_HEADS):
        a0 = acc_ref[2 * h]
        a1 = acc_ref[2 * h + 1]
        heads.append(a0[0:HEAD_DIM] / a0[HEAD_DIM:HEAD_DIM + 1]
                     - lam * (a1[0:HEAD_DIM] / a1[HEAD_DIM:HEAD_DIM + 1]))
    o = jnp.concatenate(heads, axis=0).T
    ms = _dot_split(o * o, hsum_ref[...]) * (1.0 / HEAD_DIM)
    o = o * lax.rsqrt(ms + DIFF_SUBLN_EPS) * gain_ref[...]
    o_ref[0] = (o * (1.0 - lambda_init)).astype(o_ref.dtype)


def _diff_attention(p_qk, p_vt, lamv, subln_g, lambda_init):
    B, S, _ = p_qk.shape
    t = DIFF_TILE
    W = BRANCH_WIDTH
    boff, bdiag = (jnp.asarray(b) for b in _diff_bias())
    hsum = jnp.asarray(_head_sum_mat(W, HEAD_DIM), BF16)
    gain = jnp.tile(subln_g.reshape(1, HEAD_DIM), (1, DIFF_HEADS))
    single = lambda shape: pl.BlockSpec(shape, lambda b, i: (0,) * len(shape), pipeline_mode=pl.Buffered(1))
    return pl.pallas_call(
        functools.partial(_diff_kernel, lambda_init=lambda_init),
        grid=(B, S // t),
        in_specs=[pl.BlockSpec((1, t, W), lambda b, i: (b, i, 0)), pl.BlockSpec((1, S, W), lambda b, i: (b, 0, 1)),
                  pl.BlockSpec((W, S), lambda b, i: (0, b)),
                  single(boff.shape), single(bdiag.shape), _const_spec(lamv.shape), _const_spec((1, W)),
                  _const_spec((W, W))],
        out_specs=pl.BlockSpec((1, t, W), lambda b, i: (b, i, 0)),
        out_shape=jax.ShapeDtypeStruct((B, S, W), BF16),
        scratch_shapes=[pltpu.VMEM((2 * DIFF_HEADS, 8, t), F32), pltpu.VMEM((2 * DIFF_HEADS, DIFF_ACC_ROWS, t), F32),
                        pltpu.VMEM((2, 2 * DIFF_HEADS, t, t), F32), pltpu.VMEM((2 * DIFF_HEADS, 128, t), BF16)],
        compiler_params=_params("parallel", "arbitrary"),
        name="diffattn",
    )(p_qk, p_qk, p_vt, boff, bdiag, lamv, gain, hsum)


def _conformer_tile(p, first, u_ref, w_ref, b_ref, g_ref, beta_ref, o_ref):
    ts = p.shape[0]
    Cc = CONV_CHANNELS
    halo = CONV_HALO
    rows = CONV_ROWS

    @pl.when(first)
    def _():
        u_ref[0:halo, :] = jnp.zeros((halo, Cc), F32)

    @pl.when(jnp.logical_not(first))
    def _():
        u_ref[0:halo, :] = u_ref[ts:ts + halo, :]

    u_ref[halo:halo + ts, :] = p[:, :Cc] * _sigmoid(p[:, Cc:])
    first_tap = halo - CONV_WIDTH + 1
    for r0 in range(0, ts, rows):
        acc = jnp.zeros((rows, Cc), F32) + b_ref[...]
        win = u_ref[r0:r0 + rows + halo, :]
        for b in range(8):
            wb = win if b == 0 else pltpu.roll(win, rows + halo - b, 0)
            for a in range(halo // 8 + 1):
                j = 8 * a + b - first_tap
                if 0 <= j < CONV_WIDTH:
                    acc = acc + w_ref[j:j + 1, :] * wb[8 * a:8 * a + rows]
        mu = jnp.mean(acc, axis=-1, keepdims=True)
        d = acc - mu
        var = jnp.mean(d * d, axis=-1, keepdims=True)
        y = d * lax.rsqrt(var + CONV_LN_EPS) * g_ref[...] + beta_ref[...]
        o_ref[r0:r0 + rows, :] = (y * _sigmoid(y)).astype(o_ref.dtype)


def _merge_kernel(x_ref, ya_ref, yb_ref, yc_ref, yd_ref,
                  gpre_ref, gpost_ref, wg_ref, gb_ref, wb_ref, wo_ref, out_ref):
    x = x_ref[...]
    h = _rms(x, gpre_ref[...]).astype(BF16)
    ys = (ya_ref[...], yb_ref[...], yc_ref[...], yd_ref[...])
    merged = None
    for n in range(N_BRANCHES):
        gate = _sigmoid(_dot(h, wg_ref[n]) + gb_ref[n])
        term = gate * _dot(ys[n], wb_ref[n])
        merged = term if merged is None else merged + term
    y = _dot(merged.astype(BF16), wo_ref[...])
    out_ref[...] = x + _rms(y, gpost_ref[...])


def _merge(x2, ya, yb, yc, yd, gpre, gpost, w_gate, gate_bias, w_branch, w_out):
    T = x2.shape[0]
    tm = TOKEN_TILE
    D = D_MODEL
    W = BRANCH_WIDTH
    row = lambda w: pl.BlockSpec((tm, w), lambda i: (i, 0))
    single = lambda shape: pl.BlockSpec(shape, lambda i: (0,) * len(shape), pipeline_mode=pl.Buffered(1))
    return pl.pallas_call(
        _merge_kernel,
        grid=(T // tm,),
        in_specs=[row(D)] + [row(W)] * 4 + [_const_spec((1, D)), _const_spec((1, D)),
                  single((N_BRANCHES, D, D)), _const_spec((N_BRANCHES, 1, D)), single((N_BRANCHES, W, D)),
                  single((D, D))],
        out_specs=row(D),
        out_shape=jax.ShapeDtypeStruct((T, D), F32),
        compiler_params=_params("parallel"),
        name="merge",
    )(x2, ya, yb, yc, yd, gpre, gpost, w_gate, gate_bias, w_branch, w_out)


def _ffn_kernel(x_ref, gpre_ref, gpost_ref, wg_ref, wu_ref, wd_ref, out_ref):
    x = x_ref[...]
    h = _rms(x, gpre_ref[...]).astype(BF16)
    g = _dot(h, wg_ref[...])
    u = _dot(h, wu_ref[...])
    a = (g * _sigmoid(g) * u).astype(BF16)
    f = _dot(a, wd_ref[...])
    out_ref[...] = x + _rms(f, gpost_ref[...])


def _ffn(x2, gpre, gpost, w_gate, w_up, w_down):
    T = x2.shape[0]
    tm = TOKEN_TILE
    D = D_MODEL
    row = pl.BlockSpec((tm, D), lambda i: (i, 0))
    single = lambda shape: pl.BlockSpec(shape, lambda i: (0,) * len(shape), pipeline_mode=pl.Buffered(1))
    return pl.pallas_call(
        _ffn_kernel,
        grid=(T // tm,),
        in_specs=[row, _const_spec((1, D)), _const_spec((1, D)), single((D, D_FF)), single((D, D_FF)),
                  single((D_FF, D))],
        out_specs=row,
        out_shape=jax.ShapeDtypeStruct((T, D), F32),
        compiler_params=_params("parallel"),
        name="swiglu",
    )(x2, gpre, gpost, w_gate, w_up, w_down)


def _layer(x2, B, S, l, prm):
    D = D_MODEL
    W = BRANCH_WIDTH
    row = lambda a: a.reshape(1, -1)
    w_in = prm["w_in"][l]
    w_dil = w_in[:, RWKV_COLS:RWKV_COLS + DIL_COLS].reshape(D, 3, DIL_GROUPS, W)
    w_dil = jnp.concatenate([w_dil[:, 1], w_dil[:, 2], w_dil[:, 0]], axis=-1).transpose(1, 0, 2)
    diff0 = RWKV_COLS + DIL_COLS
    w_mix = jnp.concatenate([w_in[:, :RWKV_COLS]] + [w_dil[g] for g in range(DIL_GROUPS)]
                            + [w_in[:, diff0:diff0 + DIFF_QK_COLS], w_in[:, diff0 + DIFF_COLS:MIX_COLS]],
                            axis=1).astype(BF16)
    w_vt = w_in[:, diff0 + DIFF_QK_COLS:diff0 + DIFF_COLS].T.astype(BF16)
    w_gate = w_in[:, MIX_COLS:].reshape(D, N_BRANCHES, D).transpose(1, 0, 2).astype(BF16)

    p_rwkv, pd0, pd1, pd2, p_qk, yd, p_vt = _inproj(
        x2, S, row(prm["norm_mix_pre"][l]), w_mix, w_vt, prm["conv_dw_w"][l].reshape(CONV_WIDTH, CONV_CHANNELS),
        row(prm["conv_dw_b"][l]), row(prm["conv_ln_g"][l]), row(prm["conv_ln_b"][l]))

    zeros = jnp.zeros((64, W), F32)
    wa_up = jnp.concatenate([jnp.concatenate([prm["rwkv_w_up"][l], zeros], axis=1),
                             jnp.concatenate([zeros, prm["rwkv_a_up"][l]], axis=1)], axis=0).astype(BF16)
    ya = _rwkv(p_rwkv.reshape(B, S, RWKV_COLS), row(prm["rwkv_mu"][l]), row(prm["rwkv_w0"][l]),
               row(prm["rwkv_a0"][l]), wa_up, prm["rwkv_g_up"][l].astype(BF16), row(prm["rwkv_k_k"][l]),
               row(prm["rwkv_k_a"][l]), row(prm["rwkv_r_k"][l]), row(prm["rwkv_ln_g"][l]),
               row(prm["rwkv_ln_b"][l]))

    group_w = DIL_COLS // DIL_GROUPS
    yb = _dilated(pd0.reshape(B, S, group_w), pd1.reshape(B, S, group_w), pd2.reshape(B, S, group_w))

    lambda_init = 0.8 - 0.6 * math.exp(-0.3 * l)
    lamv = jnp.stack([prm["diff_lam_q1"][l], prm["diff_lam_k1"][l], prm["diff_lam_q2"][l], prm["diff_lam_k2"][l]])
    yc = _diff_attention(p_qk.reshape(B, S, DIFF_QK_COLS), p_vt, lamv, prm["diff_subln_g"][l], lambda_init)

    x2 = _merge(x2, ya.reshape(B * S, W), yb.reshape(B * S, W), yc.reshape(B * S, W), yd.reshape(B * S, W),
                row(prm["norm_mix_pre"][l]), row(prm["norm_mix_post"][l]), w_gate,
                prm["gate_bias"][l].reshape(N_BRANCHES, 1, D), prm["w_branch"][l].astype(BF16),
                prm["w_out"][l].astype(BF16))

    return _ffn(x2, row(prm["norm_ffn_pre"][l]), row(prm["norm_ffn_post"][l]), prm["ffn_w_gate"][l].astype(BF16),
                prm["ffn_w_up"][l].astype(BF16), prm["ffn_w_down"][l].astype(BF16))


def kernel(x, norm_mix_pre, norm_mix_post, norm_ffn_pre, norm_ffn_post, w_in, gate_bias, rwkv_mu, rwkv_w0, rwkv_w_up, rwkv_a0, rwkv_a_up, rwkv_g_up, rwkv_k_k, rwkv_k_a, rwkv_r_k, rwkv_ln_g, rwkv_ln_b, diff_lam_q1, diff_lam_k1, diff_lam_q2, diff_lam_k2, diff_subln_g, conv_dw_w, conv_dw_b, conv_ln_g, conv_ln_b, w_branch, w_out, ffn_w_gate, ffn_w_up, ffn_w_down):
    prm = dict(locals())
    B, S, D = x.shape
    x2 = x.reshape(B * S, D)
    for l in range(w_in.shape[0]):
        x2 = _layer(x2, B, S, l, prm)
    return x2.reshape(B, S, D)
```

```python
import functools
import math

import numpy as np
import jax
import jax.numpy as jnp
from jax import lax
from jax.experimental import pallas as pl
from jax.experimental.pallas import tpu as pltpu

F32 = jnp.float32
BF16 = jnp.bfloat16

D_MODEL = 1024
DEPTH = 4
HEAD_DIM = 64
BRANCH_WIDTH = 256
N_BRANCHES = 4

RWKV_HEADS = 4
RWKV_WIDTH = 256
RWKV_COLS = 1024
RWKV_DECAY_SCALE = 0.606531
RWKV_LN_EPS = 64e-5
RWKV_CHUNK = 64
RWKV_TILE = 256
RWKV_SEQS = 2

DIL_PATTERNS = ((128, 1), (512, 4), (2048, 16))
DIL_GROUPS = 3
DIL_HEADS = 12
DIL_QKV = 768
DIL_COLS = 2304
DIL_BLOCK = 128
DIL_TILE = 2048
DIL_SUB = 512
DIL_G0_BLOCKS_PER_TRIP = 3
DIL_G2_BLOCKS_PER_TRIP = 4

DIFF_HEADS = 4
DIFF_QK_DIM = 32
DIFF_COLS = 768
DIFF_TILE = 512
DIFF_QK_COLS = 512
DIFF_ACC_ROWS = HEAD_DIM + 16
DIFF_SUBLN_EPS = 1e-5

CONV_CHANNELS = 256
CONV_WIDTH = 31
CONV_COLS = 512
CONV_LN_EPS = 1e-5
CONV_HALO = 32
CONV_ROWS = 64

MIX_COLS = RWKV_COLS + DIL_COLS + DIFF_COLS + CONV_COLS
D_FF = 2816
NORM_EPS = 1e-6
NEG_INF = -1e30
_LOG2E = 1.4426950408889634

VMEM_LIMIT_BYTES = 56 * 1024 * 1024
TOKEN_TILE = 512
MXU_COLS = 256


def _dot(a, b):
    return jnp.dot(a, b, preferred_element_type=F32)


def _dot_nt(a, b):
    return lax.dot_general(a, b, (((1,), (1,)), ((), ())), preferred_element_type=F32)


def _dot_tn(a, b):
    return lax.dot_general(a, b, (((0,), (0,)), ((), ())), preferred_element_type=F32)


def _dot_split(x, m):
    hi = x.astype(BF16)
    lo = (x - hi.astype(F32)).astype(BF16)
    return _dot(hi, m) + _dot(lo, m)


def _dot_split_rhs(m, x):
    hi = x.astype(BF16)
    lo = (x - hi.astype(F32)).astype(BF16)
    return _dot(m, hi) + _dot(m, lo)


def _rms(x, gain):
    return x * lax.rsqrt(jnp.mean(x * x, axis=-1, keepdims=True) + NORM_EPS) * gain


def _sigmoid(x):
    return 1.0 / (1.0 + jnp.exp(-x))


def _const_spec(shape):
    nd = len(shape)
    return pl.BlockSpec(shape, lambda *_: (0,) * nd)


def _params(*sem):
    return pltpu.CompilerParams(dimension_semantics=sem, vmem_limit_bytes=VMEM_LIMIT_BYTES)


def _inproj_kernel(x_ref, g_ref, w_ref, wvt_ref, cw_ref, cb_ref, cg_ref, cbeta_ref,
                   o_rwkv, o_dil0, o_dil1, o_dil2, o_diff, o_conv, o_vt, h_ref, u_ref, *, tiles_per_seq):
    tm = x_ref.shape[0]
    hf = _rms(x_ref[...], g_ref[...])
    h = hf.astype(BF16)
    conv0 = w_ref.shape[1] - CONV_COLS
    p_conv = jnp.concatenate([_dot(h, w_ref[:, conv0 + c0:conv0 + c0 + MXU_COLS])
                              for c0 in range(0, CONV_COLS, MXU_COLS)], axis=1)
    _conformer_tile(p_conv, pl.program_id(0) % tiles_per_seq == 0, u_ref, cw_ref, cb_ref, cg_ref, cbeta_ref, o_conv)
    slabs = D_MODEL // 128
    for j in range(slabs):
        h_ref[j] = hf[:, j * 128:(j + 1) * 128]

    def permuted(d):
        cols = [jnp.concatenate([h_ref[j, pl.ds(r, tm // d, stride=d), :] for r in range(d)], axis=0)
                for j in range(slabs)]
        return jnp.concatenate(cols, axis=1).astype(BF16)

    group_w = DIL_COLS // DIL_GROUPS
    col = 0
    for o_ref, width, lhs in ((o_rwkv, RWKV_COLS, h), (o_dil0, group_w, h),
                              (o_dil1, group_w, permuted(DIL_PATTERNS[1][1])),
                              (o_dil2, group_w, permuted(DIL_PATTERNS[2][1])),
                              (o_diff, DIFF_QK_COLS, h)):
        for c0 in range(0, width, MXU_COLS):
            y = _dot(lhs, w_ref[:, col + c0:col + c0 + MXU_COLS])
            o_ref[:, c0:c0 + MXU_COLS] = y.astype(o_ref.dtype)
        col += width
    o_vt[...] = _dot_nt(wvt_ref[...], h).astype(o_vt.dtype)


def _inproj(x2, seq_len, gain, w_mix, w_vt, dw_w, dw_b, ln_g, ln_b):
    T = x2.shape[0]
    tm = TOKEN_TILE
    assert tm == DIL_SUB and seq_len % tm == 0
    group_w = DIL_COLS // DIL_GROUPS
    W = BRANCH_WIDTH
    Cc = CONV_CHANNELS
    row = lambda w: pl.BlockSpec((tm, w), lambda i: (i, 0))
    bf = lambda w: jax.ShapeDtypeStruct((T, w), BF16)
    vec = _const_spec((1, Cc))
    return pl.pallas_call(
        functools.partial(_inproj_kernel, tiles_per_seq=seq_len // tm),
        grid=(T // tm,),
        in_specs=[row(D_MODEL), _const_spec((1, D_MODEL)), _const_spec((D_MODEL, MIX_COLS - W)),
                  _const_spec((W, D_MODEL)), _const_spec((CONV_WIDTH, Cc)), vec, vec, vec],
        out_specs=[row(RWKV_COLS), row(group_w), row(group_w), row(group_w), row(DIFF_QK_COLS), row(Cc),
                   pl.BlockSpec((W, tm), lambda i: (0, i))],
        out_shape=[jax.ShapeDtypeStruct((T, RWKV_COLS), F32), bf(group_w), bf(group_w), bf(group_w),
                   bf(DIFF_QK_COLS), bf(Cc), jax.ShapeDtypeStruct((W, T), BF16)],
        scratch_shapes=[pltpu.VMEM((D_MODEL // 128, tm, 128), F32), pltpu.VMEM((tm + CONV_HALO, Cc), F32)],
        compiler_params=_params("arbitrary"),
        name="inproj",
    )(x2, gain, w_mix, w_vt, dw_w, dw_b, ln_g, ln_b)


_MASK_STRICT, _MASK_INCL, _MASK_EYE, _MASK_LEVEL0 = 0, 1, 2, 3
_INV_LEVELS = (1, 2, 4, 8, 16, 32)


def _rwkv_masks():
    n = RWKV_HEADS * RWKV_CHUNK
    r = np.arange(n)[:, None]
    c = np.arange(n)[None, :]
    same = (r // RWKV_CHUNK) == (c // RWKV_CHUNK)
    out = [same & (r > c), same & (r >= c), r == c]
    for s in _INV_LEVELS:
        out.append((r // (2 * s) == c // (2 * s)) & (r % (2 * s) >= s) & (c % (2 * s) < s))
    head = (r // RWKV_CHUNK) == (c // HEAD_DIM)
    out.append(head)
    return np.stack(out).astype(np.float32)


_MASK_HEAD = 3 + len(_INV_LEVELS)


def _chunk_sum_mats(ts):
    r = np.arange(ts)[:, None]
    c = np.arange(ts)[None, :]
    same = (r // RWKV_CHUNK) == (c // RWKV_CHUNK)
    return np.stack([same & (c <= r), same]).astype(np.float32)


def _head_sum_mat(width, group):
    r = np.arange(width)[:, None]
    c = np.arange(width)[None, :]
    return ((r // group) == (c // group)).astype(np.float32)


def _rwkv_kernel(p_ref, mu_ref, w0_ref, a0_ref, wa_ref, gup_ref, kk_ref, ka_ref, rk_ref, lng_ref, lnb_ref,
                 masks_ref, csum_ref, hsum_ref, o_ref,
                 state_ref, prev_ref, at_ref, bt_ref, kt_ref, rt_ref, bh_ref, kh_ref, v_ref, ec_ref, oc_ref,
                 lhs_ref, upd_ref, vr_ref, arb_ref, ut_ref, arkv_ref):
    nseq, ts = p_ref.shape[0], p_ref.shape[1]
    W = RWKV_WIDTH
    C = RWKV_CHUNK
    n = RWKV_HEADS * C
    nc = ts // C

    @pl.when(pl.program_id(1) == 0)
    def _():
        state_ref[...] = jnp.zeros_like(state_ref)
        prev_ref[...] = jnp.zeros_like(prev_ref)

    hsum = hsum_ref[...]
    kept = []
    for q in range(nseq):
        p = p_ref[q]
        row = lax.broadcasted_iota(jnp.int32, p.shape, 0)
        p_prev = jnp.where(row == 0, prev_ref[q], pltpu.roll(p, 1, 0))
        prev_ref[q] = p[ts - 1:ts, :]
        p = p + (p_prev - p) * mu_ref[...]

        r = p[:, 0:W]
        k = p[:, W:2 * W]
        v = p[:, 2 * W:3 * W]
        zwa = p[:, 3 * W:3 * W + 128]
        zg = p[:, 3 * W + 128:]
        lane = lax.broadcasted_iota(jnp.int32, zwa.shape, 1)
        tz = jnp.where(lane < 64, jnp.tanh(zwa), zwa)
        wa = _dot(tz.astype(BF16), wa_ref[...])
        logw = -RWKV_DECAY_SCALE * _sigmoid(w0_ref[...] + wa[:, :W])
        alpha = _sigmoid(a0_ref[...] + wa[:, W:])
        g = _dot(_sigmoid(zg).astype(BF16), gup_ref[...])

        kk = k * kk_ref[...]
        kk = kk * lax.rsqrt(jnp.maximum(_dot_split(kk * kk, hsum), 1e-24))
        k2 = k * (1.0 + (alpha - 1.0) * ka_ref[...])
        bvec = kk * alpha

        cum = _dot_split_rhs(csum_ref[0], logw)
        cum_end = _dot_split_rhs(csum_ref[1], logw)
        e_neg = jnp.exp(-cum)
        e_rem = jnp.exp(cum_end - cum)
        at_ref[q] = -kk * jnp.exp(cum - logw)
        bt_ref[q] = bvec * e_neg
        kt_ref[q] = k2 * e_neg
        rt_ref[q] = r * jnp.exp(cum)
        bh_ref[q] = bvec * e_rem
        kh_ref[q] = k2 * e_rem
        v_ref[q] = v
        ec_ref[q] = jnp.exp(cum_end)
        kept.append((r * k2 * rk_ref[...], v, g))

    def tile4(x):
        x4 = jnp.concatenate([x, x, x, x], axis=0)
        return jnp.where(masks_ref[_MASK_HEAD] > 0, x4, 0.0).astype(BF16)

    def keep(m, x):
        return jnp.where(masks_ref[m] > 0, x, 0.0)

    chains = [(q, c) for q in range(nseq) for c in range(nc)]
    rows = [slice(c * C, (c + 1) * C) for _, c in chains]
    at = [tile4(at_ref[q, r, :]) for (q, _), r in zip(chains, rows)]
    vr = [tile4(v_ref[q, r, :]) for (q, _), r in zip(chains, rows)]
    a_ab, akv = [], []
    for j, ((q, c), r) in enumerate(zip(chains, rows)):
        rt = tile4(rt_ref[q, r, :])
        lhs = jnp.concatenate([at[j], rt], axis=0)
        rhs = jnp.concatenate([tile4(bt_ref[q, r, :]), tile4(kt_ref[q, r, :])], axis=0)
        a = _dot_nt(lhs, rhs)
        a_ab.append(keep(_MASK_STRICT, a[:n, :n]))
        arb_ref[q, c] = keep(_MASK_INCL, a[n:, :n]).astype(BF16)
        akv_arkv = _dot(jnp.concatenate([keep(_MASK_STRICT, a[:n, n:]), keep(_MASK_INCL, a[n:, n:])],
                                        axis=0).astype(BF16), vr[j])
        akv.append(akv_arkv[:n].astype(BF16))
        arkv_ref[q, c] = akv_arkv[n:]
        lhs_ref[q, c, n:2 * n, :] = rt
        vr_ref[q, c] = vr[j]
        upd_ref[q, c, 0:n, :] = tile4(bh_ref[q, r, :])
        upd_ref[q, c, n:2 * n, :] = tile4(kh_ref[q, r, :])

    def lower_rows(x, s):
        return jnp.concatenate([x[k * 2 * s + s:(k + 1) * 2 * s] for k in range(n // (2 * s))], axis=0)

    def scatter_lower(y, s):
        zero = jnp.zeros((s, y.shape[1]), y.dtype)
        return jnp.concatenate([p for k in range(n // (2 * s)) for p in (zero, y[k * s:(k + 1) * s])], axis=0)

    t = [masks_ref[_MASK_EYE] + keep(_MASK_LEVEL0, a) for a in a_ab]
    for li in range(1, len(_INV_LEVELS)):
        s = _INV_LEVELS[li]
        tb = [x.astype(BF16) for x in t]
        x_lvl = [keep(_MASK_LEVEL0 + li, a) for a in a_ab]
        if s % 8 == 0:
            xt = [scatter_lower(_dot(lower_rows(x, s).astype(BF16), y), s).astype(BF16) for x, y in zip(x_lvl, tb)]
            t = [x + scatter_lower(_dot(lower_rows(x, s).astype(BF16), z), s) for x, z in zip(t, xt)]
        else:
            xt = [_dot(x.astype(BF16), y).astype(BF16) for x, y in zip(x_lvl, tb)]
            t = [x + _dot(y, z) for x, y, z in zip(t, tb, xt)]
    for j, (q, c) in enumerate(chains):
        tb = t[j].astype(BF16)
        lhs_ref[q, c, 0:n, :] = _dot(tb, at[j]).astype(BF16)
        ut_ref[q, c] = _dot(tb, akv[j])

    for c in range(nc):
        s = [state_ref[q] for q in range(nseq)]
        g_us = [_dot_nt(lhs_ref[q, c], s[q].astype(BF16)) for q in range(nseq)]
        ub = [(g_us[q][:n] + ut_ref[q, c]).astype(BF16) for q in range(nseq)]
        o = [g_us[q][n:] + _dot(arb_ref[q, c], ub[q]) + arkv_ref[q, c] for q in range(nseq)]
        upd = [_dot_tn(jnp.concatenate([ub[q], vr_ref[q, c]], axis=0), upd_ref[q, c]) for q in range(nseq)]
        for q in range(nseq):
            oc_ref[q, c * C:(c + 1) * C, :] = o[q][0:C] + o[q][C:2 * C] + o[q][2 * C:3 * C] + o[q][3 * C:4 * C]
            state_ref[q] = s[q] * ec_ref[q, c * C:c * C + 1, :] + upd[q]

    hmean = hsum * (1.0 / HEAD_DIM)
    for q, (rk2, v, g) in enumerate(kept):
        o = oc_ref[q]
        mean = _dot_split(o, hmean)
        d = o - mean
        var = _dot_split(d * d, hmean)
        o = d * lax.rsqrt(var + RWKV_LN_EPS) * lng_ref[...] + lnb_ref[...]
        bonus = _dot_split(rk2, hsum)
        o_ref[q] = ((o + bonus * v) * g).astype(o_ref.dtype)


def _rwkv(p, mu, w0, a0, wa_up, g_up, k_k, k_a, r_k, ln_g, ln_b):
    B, S, _ = p.shape
    ts = RWKV_TILE
    nseq = RWKV_SEQS
    assert B % nseq == 0 and S % ts == 0
    W = RWKV_WIDTH
    n = RWKV_HEADS * RWKV_CHUNK
    nc = ts // RWKV_CHUNK
    masks = jnp.asarray(_rwkv_masks())
    csum = jnp.asarray(_chunk_sum_mats(ts), BF16)
    hsum = jnp.asarray(_head_sum_mat(W, HEAD_DIM), BF16)
    vec = _const_spec((1, W))
    buf = lambda: pltpu.VMEM((nseq, ts, W), F32)
    return pl.pallas_call(
        _rwkv_kernel,
        grid=(B // nseq, S // ts),
        in_specs=[pl.BlockSpec((nseq, ts, RWKV_COLS), lambda b, s: (b, s, 0)),
                  _const_spec((1, RWKV_COLS)), vec, vec, _const_spec((128, 2 * W)), _const_spec((128, W)),
                  vec, vec, vec, vec, vec,
                  _const_spec(masks.shape), _const_spec(csum.shape), _const_spec(hsum.shape)],
        out_specs=pl.BlockSpec((nseq, ts, W), lambda b, s: (b, s, 0)),
        out_shape=jax.ShapeDtypeStruct((B, S, W), BF16),
        scratch_shapes=[pltpu.VMEM((nseq, n, n), F32), pltpu.VMEM((nseq, 1, RWKV_COLS), F32)]
                       + [buf() for _ in range(9)]
                       + [pltpu.VMEM((nseq, nc, 2 * n, n), BF16), pltpu.VMEM((nseq, nc, 2 * n, n), BF16),
                          pltpu.VMEM((nseq, nc, n, n), BF16), pltpu.VMEM((nseq, nc, n, n), BF16),
                          pltpu.VMEM((nseq, nc, n, n), F32), pltpu.VMEM((nseq, nc, n, n), F32)],
        compiler_params=_params("parallel", "arbitrary"),
        name="rwkv7",
    )(p, mu, w0, a0, wa_up, g_up, k_k, k_a, r_k, ln_g, ln_b, masks, csum, hsum)


def _alibi_slopes(n):
    return 2.0 ** (-8.0 * np.arange(1, n + 1) / n)


def _dil_bias(group):
    window, dilation = DIL_PATTERNS[group]
    n = window // dilation
    slopes = _alibi_slopes(DIL_HEADS).reshape(DIL_GROUPS, -1)[group]
    rel = n + np.arange(n)[:, None] - np.arange(2 * n)[None, :]
    valid = (rel >= 0) & (rel <= n)
    bias = -slopes[:, None, None] * (dilation * rel).astype(np.float64)[None]
    return np.where(valid[None], bias, NEG_INF).astype(np.float32)


def _dil_attend(blocks, bias_of, front):
    shape = blocks[0][0].shape
    heads = shape[1] // HEAD_DIM
    head = lax.broadcasted_iota(jnp.int32, shape, 1) // HEAD_DIM
    pairs = [(b, h) for b in range(len(blocks)) for h in range(heads)]
    s = [_dot_nt(jnp.where(head == h, blocks[b][0], jnp.zeros(shape, BF16)), blocks[b][1]) + bias_of(h)
         for b, h in pairs]
    if front is not None:
        s = [jnp.where(front, NEG_INF, x) for x in s]
    m = [jnp.max(x, axis=-1, keepdims=True) for x in s]
    e = [jnp.exp2(x - mx) for x, mx in zip(s, m)]
    l = [jnp.sum(x, axis=-1, keepdims=True) for x in e]
    per_half = 128 // HEAD_DIM
    pv = [_dot(x.astype(BF16), blocks[b][2][:, (h // per_half) * 128:(h // per_half + 1) * 128])
          for x, (b, h) in zip(e, pairs)]
    low = lax.broadcasted_iota(jnp.int32, (shape[0], 128), 1) < HEAD_DIM
    results = []
    for b in range(len(blocks)):
        outs, lses = [], []
        for half in range(heads // per_half):
            i0, i1 = b * heads + half * per_half, b * heads + half * per_half + 1
            outs.append(jnp.where(low, pv[i0] / l[i0], pv[i1] / l[i1]))
            lses.append(jnp.where(low, m[i0] + jnp.log(l[i0]) * _LOG2E, m[i1] + jnp.log(l[i1]) * _LOG2E))
        results.append((outs, lses))
    return results


def _dil_kernel(c0_ref, c1_ref, c2_ref, p0_ref, p1_ref, p2_ref, b0_ref, b1_ref, b2_ref, y_ref, og_ref, lg_ref):
    n = DIL_BLOCK
    W = BRANCH_WIDTH
    sub = DIL_SUB
    first = pl.program_id(1) == 0
    front = jnp.logical_and(first, lax.broadcasted_iota(jnp.int32, (n, 2 * n), 1) < n)
    qscale = HEAD_DIM ** -0.5 * _LOG2E
    kcols, vcols, qcols = slice(0, W), slice(W, 2 * W), slice(2 * W, 3 * W)

    def scaled(q):
        return (q.astype(F32) * qscale).astype(BF16)

    def put(g, rows, out, lse):
        for half in range(W // 128):
            og_ref[g, half, rows, :] = out[half]
            lg_ref[g, half, rows, :] = lse[half]

    def block_of(prev, cur):
        return (scaled(cur[:, qcols]), jnp.concatenate([prev[:, kcols], cur[:, kcols]], axis=0),
                jnp.concatenate([prev[:, vcols], cur[:, vcols]], axis=0))

    (out, lse), = _dil_attend([block_of(p0_ref[0], c0_ref[0, 0:n, :])], lambda h: b0_ref[h], front)
    put(0, pl.ds(0, n), out, lse)
    per0 = DIL_G0_BLOCKS_PER_TRIP

    def g0_body(i, carry):
        blocks, rows = [], []
        for j in range(per0):
            r0 = pl.multiple_of((i * per0 + j) * n, n)
            kv = c0_ref[0, pl.ds(r0, 2 * n), 0:2 * W]
            blocks.append((scaled(c0_ref[0, pl.ds(r0 + n, n), qcols]), kv[:, kcols], kv[:, vcols]))
            rows.append(pl.ds(r0 + n, n))
        for r, (out, lse) in zip(rows, _dil_attend(blocks, lambda h: b0_ref[h], None)):
            put(0, r, out, lse)
        return carry

    assert (DIL_TILE // n - 1) % per0 == 0
    lax.fori_loop(0, (DIL_TILE // n - 1) // per0, g0_body, 0)

    d1 = DIL_PATTERNS[1][1]

    def g1_rows(u, r):
        return pl.ds(pl.multiple_of(u * sub + r * n, n), n)

    def g1_run(u, prevs, front_mask):
        blocks = [block_of(prevs[r], c1_ref[0, g1_rows(u, r), :]) for r in range(d1)]
        for r, (out, lse) in enumerate(_dil_attend(blocks, lambda h: b1_ref[h], front_mask)):
            put(1, pl.ds(u * sub + r, n, stride=d1), out, lse)

    g1_run(0, [p1_ref[0, r * n:(r + 1) * n, :] for r in range(d1)], front)

    def g1_body(u, carry):
        g1_run(u, [c1_ref[0, g1_rows(u - 1, r), 0:2 * W] for r in range(d1)], None)
        return carry

    lax.fori_loop(1, DIL_TILE // sub, g1_body, 0)

    d2 = DIL_PATTERNS[2][1]
    piece = sub // d2
    per2 = DIL_G2_BLOCKS_PER_TRIP

    def g2_body(i, carry):
        def rows_of(ref, r, cols):
            return jnp.concatenate(
                [ref[0, pl.ds(pl.multiple_of(u * sub + r * piece, piece), piece), cols]
                 for u in range(DIL_TILE // sub)], axis=0)
        rs = [i * per2 + j for j in range(per2)]
        blocks = [block_of(rows_of(p2_ref, r, slice(0, 2 * W)), rows_of(c2_ref, r, slice(0, 3 * W))) for r in rs]
        for r, (out, lse) in zip(rs, _dil_attend(blocks, lambda h: b2_ref[h], front)):
            put(2, pl.ds(r, n, stride=d2), out, lse)
        return carry

    assert d2 % per2 == 0
    lax.fori_loop(0, d2 // per2, g2_body, 0)

    rows_per = 256

    def combine(j, carry):
        rows = pl.ds(pl.multiple_of(j * rows_per, rows_per), rows_per)
        for half in range(W // 128):
            ls = [lg_ref[g, half, rows, :] for g in range(DIL_GROUPS)]
            m = jnp.maximum(jnp.maximum(ls[0], ls[1]), ls[2])
            es = [jnp.exp2(x - m) for x in ls]
            num = es[0] * og_ref[0, half, rows, :] + es[1] * og_ref[1, half, rows, :] + es[2] * og_ref[2, half, rows, :]
            y_ref[0, rows, half * 128:(half + 1) * 128] = (num / (es[0] + es[1] + es[2])).astype(y_ref.dtype)
        return carry

    lax.fori_loop(0, DIL_TILE // rows_per, combine, 0)


def _dilated(pd0, pd1, pd2):
    B, S, _ = pd0.shape
    W = BRANCH_WIDTH
    T = DIL_TILE
    assert S % T == 0
    for (window, d) in DIL_PATTERNS:
        assert window // d == DIL_BLOCK
    assert DIL_PATTERNS[0][1] == 1 and DIL_PATTERNS[1][1] * DIL_BLOCK == DIL_SUB and DIL_PATTERNS[2][1] * DIL_BLOCK == T
    biases = [jnp.asarray(_dil_bias(g) * _LOG2E) for g in range(DIL_GROUPS)]
    cur = pl.BlockSpec((1, T, 3 * W), lambda b, i: (b, i, 0))
    prev = lambda rows: pl.BlockSpec((1, rows, 2 * W), lambda b, i: (b, jnp.maximum(i * (T // rows) - 1, 0), 0))
    return pl.pallas_call(
        _dil_kernel,
        grid=(B, S // T),
        in_specs=[cur, cur, cur, prev(DIL_BLOCK), prev(DIL_SUB), prev(T)] + [_const_spec(b.shape) for b in biases],
        out_specs=pl.BlockSpec((1, T, W), lambda b, i: (b, i, 0)),
        out_shape=jax.ShapeDtypeStruct((B, S, W), BF16),
        scratch_shapes=[pltpu.VMEM((DIL_GROUPS, W // 128, T, 128), F32)] * 2,
        compiler_params=_params("parallel", "arbitrary"),
        name="dilated",
    )(pd0, pd1, pd2, pd0, pd1, pd2, *biases)


def _diff_bias():
    t = DIFF_TILE
    slopes = _alibi_slopes(DIFF_HEADS) * _LOG2E
    rel = (np.arange(t)[None, :] - np.arange(t)[:, None]).astype(np.float64)
    off = -slopes[:, None, None] * rel[None]
    diag = np.where(rel[None] >= 0, off, NEG_INF)
    return off.astype(np.float32), diag.astype(np.float32)


def _diff_kernel(q_ref, k_ref, vt_ref, boff_ref, bdiag_ref, lamv_ref, gain_ref, hsum_ref, o_ref, m_ref, acc_ref,
                 s_ref, qt_ref, *, lambda_init):
    t = DIFF_TILE
    n_maps = 2 * DIFF_HEADS
    ones_rows = DIFF_ACC_ROWS - HEAD_DIM
    qi = pl.program_id(1)
    q = q_ref[0].astype(F32) * (DIFF_QK_DIM ** -0.5 * _LOG2E)
    maps_per_half = 128 // DIFF_QK_DIM
    lane = lax.broadcasted_iota(jnp.int32, (t, 128), 1)
    for c in range(n_maps):
        half = c // maps_per_half
        qh = jnp.where(lane // DIFF_QK_DIM == c % maps_per_half, q[:, half * 128:(half + 1) * 128], 0.0)
        qt_ref[c] = qh.T.astype(BF16)
    slopes = _alibi_slopes(DIFF_HEADS) * _LOG2E

    def scores(k, c):
        half = c // maps_per_half
        return _dot(k[:, half * 128:(half + 1) * 128], qt_ref[c])

    m_ref[...] = jnp.full(m_ref.shape, NEG_INF, F32)
    acc_ref[...] = jnp.zeros_like(acc_ref)

    def keys(kb):
        return pl.ds(pl.multiple_of(kb * t, t), t)

    def fold(kb, slot, bias_ref, k_next):
        ones = jnp.ones((ones_rows, t), BF16)
        vt = [jnp.concatenate([vt_ref[h * HEAD_DIM:(h + 1) * HEAD_DIM, keys(kb)], ones], axis=0)
              for h in range(DIFF_HEADS)]
        shift = ((qi - kb) * t).astype(F32)
        for c in range(n_maps):
            if k_next is not None:
                s_ref[1 - slot, c] = scores(k_next, c)
            block_bias = float(slopes[c // 2]) * shift
            s = s_ref[slot, c] + bias_ref[c // 2]
            m_prev = m_ref[c, 0:1, :]
            m_new = jnp.maximum(m_prev, jnp.max(s, axis=0, keepdims=True) - block_bias)
            e = jnp.exp2(s - (m_new + block_bias)).astype(BF16)
            m_ref[c] = jnp.broadcast_to(m_new, m_ref.shape[1:])
            acc_ref[c] = acc_ref[c] * jnp.exp2(m_prev - m_new) + _dot(vt[c // 2], e)

    k0 = k_ref[0, keys(0), :]
    for c in range(n_maps):
        s_ref[0, c] = scores(k0, c)

    def pair(jj, carry):
        fold(2 * jj, 0, boff_ref, k_ref[0, keys(2 * jj + 1), :])
        fold(2 * jj + 1, 1, boff_ref, k_ref[0, keys(2 * jj + 2), :])
        return carry

    lax.fori_loop(0, qi // 2, pair, 0)

    @pl.when(qi % 2 == 1)
    def _():
        fold(qi - 1, 0, boff_ref, k_ref[0, keys(qi), :])
        fold(qi, 1, bdiag_ref, None)

    @pl.when(qi % 2 == 0)
    def _():
        fold(qi, 0, bdiag_ref, None)

    lamv = lamv_ref[...]
    lam = (jnp.exp(jnp.sum(lamv[0:1] * lamv[1:2], axis=-1, keepdims=True))
           - jnp.exp(jnp.sum(lamv[2:3] * lamv[3:4], axis=-1, keepdims=True)) + lambda_init)
    heads = []
    for h in range(DIFF_HEADS):
        a0 = acc_ref[2 * h]
        a1 = acc_ref[2 * h + 1]
        heads.append(a0[0:HEAD_DIM] / a0[HEAD_DIM:HEAD_DIM + 1]
                     - lam * (a1[0:HEAD_DIM] / a1[HEAD_DIM:HEAD_DIM + 1]))
    o = jnp.concatenate(heads, axis=0).T
    ms = _dot_split(o * o, hsum_ref[...]) * (1.0 / HEAD_DIM)
    o = o * lax.rsqrt(ms + DIFF_SUBLN_EPS) * gain_ref[...]
    o_ref[0] = (o * (1.0 - lambda_init)).astype(o_ref.dtype)


def _diff_attention(p_qk, p_vt, lamv, subln_g, lambda_init):
    B, S, _ = p_qk.shape
    t = DIFF_TILE
    W = BRANCH_WIDTH
    boff, bdiag = (jnp.asarray(b) for b in _diff_bias())
    hsum = jnp.asarray(_head_sum_mat(W, HEAD_DIM), BF16)
    gain = jnp.tile(subln_g.reshape(1, HEAD_DIM), (1, DIFF_HEADS))
    single = lambda shape: pl.BlockSpec(shape, lambda b, i: (0,) * len(shape), pipeline_mode=pl.Buffered(1))
    return pl.pallas_call(
        functools.partial(_diff_kernel, lambda_init=lambda_init),
        grid=(B, S // t),
        in_specs=[pl.BlockSpec((1, t, W), lambda b, i: (b, i, 0)), pl.BlockSpec((1, S, W), lambda b, i: (b, 0, 1)),
                  pl.BlockSpec((W, S), lambda b, i: (0, b)),
                  single(boff.shape), single(bdiag.shape), _const_spec(lamv.shape), _const_spec((1, W)),
                  _const_spec((W, W))],
        out_specs=pl.BlockSpec((1, t, W), lambda b, i: (b, i, 0)),
        out_shape=jax.ShapeDtypeStruct((B, S, W), BF16),
        scratch_shapes=[pltpu.VMEM((2 * DIFF_HEADS, 8, t), F32), pltpu.VMEM((2 * DIFF_HEADS, DIFF_ACC_ROWS, t), F32),
                        pltpu.VMEM((2, 2 * DIFF_HEADS, t, t), F32), pltpu.VMEM((2 * DIFF_HEADS, 128, t), BF16)],
        compiler_params=_params("parallel", "arbitrary"),
        name="diffattn",
    )(p_qk, p_qk, p_vt, boff, bdiag, lamv, gain, hsum)


def _conformer_tile(p, first, u_ref, w_ref, b_ref, g_ref, beta_ref, o_ref):
    ts = p.shape[0]
    Cc = CONV_CHANNELS
    halo = CONV_HALO
    rows = CONV_ROWS

    @pl.when(first)
    def _():
        u_ref[0:halo, :] = jnp.zeros((halo, Cc), F32)

    @pl.when(jnp.logical_not(first))
    def _():
        u_ref[0:halo, :] = u_ref[ts:ts + halo, :]

    u_ref[halo:halo + ts, :] = p[:, :Cc] * _sigmoid(p[:, Cc:])
    first_tap = halo - CONV_WIDTH + 1
    for r0 in range(0, ts, rows):
        halves = []
        for lanes in (slice(0, 128), slice(128, Cc)):
            acc = jnp.zeros((rows, 128), F32) + b_ref[:, lanes]
            win = u_ref[r0:r0 + rows + halo, lanes]
            for b in range(8):
                wb = win if b == 0 else pltpu.roll(win, rows + halo - b, 0)
                for a in range(halo // 8 + 1):
                    j = 8 * a + b - first_tap
                    if 0 <= j < CONV_WIDTH:
                        acc = acc + w_ref[j:j + 1, lanes] * wb[8 * a:8 * a + rows]
            halves.append(acc)
        acc = jnp.concatenate(halves, axis=1)
        mu = jnp.mean(acc, axis=-1, keepdims=True)
        d = acc - mu
        var = jnp.mean(d * d, axis=-1, keepdims=True)
        y = d * lax.rsqrt(var + CONV_LN_EPS) * g_ref[...] + beta_ref[...]
        o_ref[r0:r0 + rows, :] = (y * _sigmoid(y)).astype(o_ref.dtype)


def _merge_kernel(x_ref, ya_ref, yb_ref, yc_ref, yd_ref,
                  gpre_ref, gpost_ref, wg_ref, gb_ref, wb_ref, wo_ref, out_ref):
    x = x_ref[...]
    h = _rms(x, gpre_ref[...]).astype(BF16)
    ys = (ya_ref[...], yb_ref[...], yc_ref[...], yd_ref[...])
    merged = None
    for n in range(N_BRANCHES):
        gate = _sigmoid(_dot(h, wg_ref[:, n * D_MODEL:(n + 1) * D_MODEL]) + gb_ref[n])
        term = gate * _dot(ys[n], wb_ref[n])
        merged = term if merged is None else merged + term
    y = _dot(merged.astype(BF16), wo_ref[...])
    out_ref[...] = x + _rms(y, gpost_ref[...])


def _merge(x2, ya, yb, yc, yd, gpre, gpost, w_gate, gate_bias, w_branch, w_out):
    T = x2.shape[0]
    tm = TOKEN_TILE
    D = D_MODEL
    W = BRANCH_WIDTH
    row = lambda w: pl.BlockSpec((tm, w), lambda i: (i, 0))
    single = lambda shape: pl.BlockSpec(shape, lambda i: (0,) * len(shape), pipeline_mode=pl.Buffered(1))
    return pl.pallas_call(
        _merge_kernel,
        grid=(T // tm,),
        in_specs=[row(D)] + [row(W)] * 4 + [_const_spec((1, D)), _const_spec((1, D)),
                  single((D, N_BRANCHES * D)), _const_spec((N_BRANCHES, 1, D)), single((N_BRANCHES, W, D)),
                  single((D, D))],
        out_specs=row(D),
        out_shape=jax.ShapeDtypeStruct((T, D), F32),
        compiler_params=_params("parallel"),
        name="merge",
    )(x2, ya, yb, yc, yd, gpre, gpost, w_gate, gate_bias, w_branch, w_out)


def _ffn_kernel(x_ref, gpre_ref, gpost_ref, wg_ref, wu_ref, wd_ref, out_ref):
    x = x_ref[...]
    h = _rms(x, gpre_ref[...]).astype(BF16)
    g = _dot(h, wg_ref[...])
    u = _dot(h, wu_ref[...])
    a = (g * _sigmoid(g) * u).astype(BF16)
    f = _dot(a, wd_ref[...])
    out_ref[...] = x + _rms(f, gpost_ref[...])


def _ffn(x2, gpre, gpost, w_gate, w_up, w_down):
    T = x2.shape[0]
    tm = TOKEN_TILE
    D = D_MODEL
    row = pl.BlockSpec((tm, D), lambda i: (i, 0))
    single = lambda shape: pl.BlockSpec(shape, lambda i: (0,) * len(shape), pipeline_mode=pl.Buffered(1))
    return pl.pallas_call(
        _ffn_kernel,
        grid=(T // tm,),
        in_specs=[row, _const_spec((1, D)), _const_spec((1, D)), single((D, D_FF)), single((D, D_FF)),
                  single((D_FF, D))],
        out_specs=row,
        out_shape=jax.ShapeDtypeStruct((T, D), F32),
        compiler_params=_params("parallel"),
        name="swiglu",
    )(x2, gpre, gpost, w_gate, w_up, w_down)


def _layer(x2, B, S, l, prm):
    D = D_MODEL
    W = BRANCH_WIDTH
    row = lambda a: a.reshape(1, -1)
    w_in = prm["w_in"][l]
    w_dil = w_in[:, RWKV_COLS:RWKV_COLS + DIL_COLS].reshape(D, 3, DIL_GROUPS, W)
    w_dil = jnp.concatenate([w_dil[:, 1], w_dil[:, 2], w_dil[:, 0]], axis=-1).transpose(1, 0, 2)
    diff0 = RWKV_COLS + DIL_COLS
    w_mix = jnp.concatenate([w_in[:, :RWKV_COLS]] + [w_dil[g] for g in range(DIL_GROUPS)]
                            + [w_in[:, diff0:diff0 + DIFF_QK_COLS], w_in[:, diff0 + DIFF_COLS:MIX_COLS]],
                            axis=1).astype(BF16)
    w_vt = w_in[:, diff0 + DIFF_QK_COLS:diff0 + DIFF_COLS].T.astype(BF16)
    w_gate = w_in[:, MIX_COLS:].astype(BF16)

    p_rwkv, pd0, pd1, pd2, p_qk, yd, p_vt = _inproj(
        x2, S, row(prm["norm_mix_pre"][l]), w_mix, w_vt, prm["conv_dw_w"][l].reshape(CONV_WIDTH, CONV_CHANNELS),
        row(prm["conv_dw_b"][l]), row(prm["conv_ln_g"][l]), row(prm["conv_ln_b"][l]))

    zeros = jnp.zeros((64, W), F32)
    wa_up = jnp.concatenate([jnp.concatenate([prm["rwkv_w_up"][l], zeros], axis=1),
                             jnp.concatenate([zeros, prm["rwkv_a_up"][l]], axis=1)], axis=0).astype(BF16)
    ya = _rwkv(p_rwkv.reshape(B, S, RWKV_COLS), row(prm["rwkv_mu"][l]), row(prm["rwkv_w0"][l]),
               row(prm["rwkv_a0"][l]), wa_up, prm["rwkv_g_up"][l].astype(BF16), row(prm["rwkv_k_k"][l]),
               row(prm["rwkv_k_a"][l]), row(prm["rwkv_r_k"][l]), row(prm["rwkv_ln_g"][l]),
               row(prm["rwkv_ln_b"][l]))

    group_w = DIL_COLS // DIL_GROUPS
    yb = _dilated(pd0.reshape(B, S, group_w), pd1.reshape(B, S, group_w), pd2.reshape(B, S, group_w))

    lambda_init = 0.8 - 0.6 * math.exp(-0.3 * l)
    lamv = jnp.stack([prm["diff_lam_q1"][l], prm["diff_lam_k1"][l], prm["diff_lam_q2"][l], prm["diff_lam_k2"][l]])
    yc = _diff_attention(p_qk.reshape(B, S, DIFF_QK_COLS), p_vt, lamv, prm["diff_subln_g"][l], lambda_init)

    x2 = _merge(x2, ya.reshape(B * S, W), yb.reshape(B * S, W), yc.reshape(B * S, W), yd.reshape(B * S, W),
                row(prm["norm_mix_pre"][l]), row(prm["norm_mix_post"][l]), w_gate,
                prm["gate_bias"][l].reshape(N_BRANCHES, 1, D), prm["w_branch"][l].astype(BF16),
                prm["w_out"][l].astype(BF16))

    return _ffn(x2, row(prm["norm_ffn_pre"][l]), row(prm["norm_ffn_post"][l]), prm["ffn_w_gate"][l].astype(BF16),
                prm["ffn_w_up"][l].astype(BF16), prm["ffn_w_down"][l].astype(BF16))


def kernel(x, norm_mix_pre, norm_mix_post, norm_ffn_pre, norm_ffn_post, w_in, gate_bias, rwkv_mu, rwkv_w0, rwkv_w_up, rwkv_a0, rwkv_a_up, rwkv_g_up, rwkv_k_k, rwkv_k_a, rwkv_r_k, rwkv_ln_g, rwkv_ln_b, diff_lam_q1, diff_lam_k1, diff_lam_q2, diff_lam_k2, diff_subln_g, conv_dw_w, conv_dw_b, conv_ln_g, conv_ln_b, w_branch, w_out, ffn_w_gate, ffn_w_up, ffn_w_down):
    prm = dict(locals())
    B, S, D = x.shape
    x2 = x.reshape(B * S, D)
    for l in range(w_in.shape[0]):
        x2 = _layer(x2, B, S, l, prm)
    return x2.reshape(B, S, D)
```

```python
import functools
import math

import numpy as np
import jax
import jax.numpy as jnp
from jax import lax
from jax.experimental import pallas as pl
from jax.experimental.pallas import tpu as pltpu

F32 = jnp.float32
BF16 = jnp.bfloat16
LANES = 128

D_MODEL = 1024
HEAD_DIM = 64
BRANCH_WIDTH = 256
N_BRANCHES = 4

RWKV_HEADS = 4
RWKV_WIDTH = 256
RWKV_COLS = 1024
RWKV_DECAY_SCALE = 0.606531
RWKV_LN_EPS = 64e-5
RWKV_DECAY_RANK = 64
RWKV_ICL_RANK = 64
RWKV_GATE_RANK = 128
RWKV_LORA_COLS = RWKV_DECAY_RANK + RWKV_ICL_RANK
RWKV_CHUNK = 64
RWKV_TILE = 256
RWKV_SEQS = 2

DIL_PATTERNS = ((128, 1), (512, 4), (2048, 16))
DIL_GROUPS = 3
DIL_HEADS = 12
DIL_COLS = 2304
DIL_BLOCK = 128
DIL_TILE = 2048
DIL_SUB = 512
DIL_G0_BLOCKS_PER_TRIP = 5
DIL_G2_BLOCKS_PER_TRIP = 8

DIFF_HEADS = 4
DIFF_QK_DIM = 32
DIFF_COLS = 768
DIFF_TILE = 512
DIFF_QK_COLS = 512
DIFF_ACC_ROWS = HEAD_DIM + 16
DIFF_SUBLN_EPS = 1e-5

CONV_CHANNELS = 256
CONV_WIDTH = 31
CONV_COLS = 512
CONV_LN_EPS = 1e-5
CONV_HALO = 32
CONV_ROWS = 64

MIX_COLS = RWKV_COLS + DIL_COLS + DIFF_COLS + CONV_COLS
D_FF = 2816
NORM_EPS = 1e-6
NEG_INF = -1e30
_LOG2E = 1.4426950408889634

VMEM_LIMIT_BYTES = 56 * 1024 * 1024
TOKEN_TILE = 512


def _dot(a, b):
    return jnp.dot(a, b, preferred_element_type=F32)


def _dot_nt(a, b):
    return lax.dot_general(a, b, (((1,), (1,)), ((), ())), preferred_element_type=F32)


def _dot_tn(a, b):
    return lax.dot_general(a, b, (((0,), (0,)), ((), ())), preferred_element_type=F32)


def _dot_split(x, m):
    hi = x.astype(BF16)
    lo = (x - hi.astype(F32)).astype(BF16)
    return _dot(hi, m) + _dot(lo, m)


def _dot_split_rhs(m, x):
    hi = x.astype(BF16)
    lo = (x - hi.astype(F32)).astype(BF16)
    return _dot(m, hi) + _dot(m, lo)


def _rms(x, gain):
    return x * lax.rsqrt(jnp.mean(x * x, axis=-1, keepdims=True) + NORM_EPS) * gain


def _sigmoid(x):
    return 1.0 / (1.0 + jnp.exp(-x))


def _const_spec(shape):
    nd = len(shape)
    return pl.BlockSpec(shape, lambda *_: (0,) * nd)


def _params(*sem):
    return pltpu.CompilerParams(dimension_semantics=sem, vmem_limit_bytes=VMEM_LIMIT_BYTES)


def _inproj_kernel(x_ref, g_ref, w_ref, wvt_ref, cw_ref, cb_ref, cg_ref, cbeta_ref,
                   o_rwkv, o_dil0, o_dil1, o_dil2, o_diff, o_conv, o_vt, h_ref, u_ref, *, tiles_per_seq):
    tm = x_ref.shape[0]
    hf = _rms(x_ref[...], g_ref[...])
    h = hf.astype(BF16)
    conv0 = w_ref.shape[1] - CONV_COLS
    _conformer_tile(_dot(h, w_ref[:, conv0:]), pl.program_id(0) % tiles_per_seq == 0, u_ref, cw_ref, cb_ref, cg_ref,
                    cbeta_ref, o_conv)
    slabs = D_MODEL // LANES
    for j in range(slabs):
        h_ref[j] = hf[:, j * LANES:(j + 1) * LANES]

    def permuted(d):
        cols = [jnp.concatenate([h_ref[j, pl.ds(r, tm // d, stride=d), :] for r in range(d)], axis=0)
                for j in range(slabs)]
        return jnp.concatenate(cols, axis=1).astype(BF16)

    group_w = DIL_COLS // DIL_GROUPS
    col = 0
    for o_ref, width, lhs in ((o_rwkv, RWKV_COLS, h), (o_dil0, group_w, h),
                              (o_dil1, group_w, permuted(DIL_PATTERNS[1][1])),
                              (o_dil2, group_w, permuted(DIL_PATTERNS[2][1])),
                              (o_diff, DIFF_QK_COLS, h)):
        o_ref[...] = _dot(lhs, w_ref[:, col:col + width]).astype(o_ref.dtype)
        col += width
    o_vt[...] = _dot_nt(wvt_ref[...], h).astype(o_vt.dtype)


def _inproj(x2, seq_len, gain, w_mix, w_vt, dw_w, dw_b, ln_g, ln_b):
    T = x2.shape[0]
    tm = TOKEN_TILE
    assert tm == DIL_SUB and seq_len % tm == 0
    group_w = DIL_COLS // DIL_GROUPS
    W = BRANCH_WIDTH
    Cc = CONV_CHANNELS
    row = lambda w: pl.BlockSpec((tm, w), lambda i: (i, 0))
    bf = lambda w: jax.ShapeDtypeStruct((T, w), BF16)
    vec = _const_spec((1, Cc))
    return pl.pallas_call(
        functools.partial(_inproj_kernel, tiles_per_seq=seq_len // tm),
        grid=(T // tm,),
        in_specs=[row(D_MODEL), _const_spec((1, D_MODEL)), _const_spec((D_MODEL, MIX_COLS - W)),
                  _const_spec((W, D_MODEL)), _const_spec((CONV_WIDTH, Cc)), vec, vec, vec],
        out_specs=[row(RWKV_COLS), row(group_w), row(group_w), row(group_w), row(DIFF_QK_COLS), row(Cc),
                   pl.BlockSpec((W, tm), lambda i: (0, i))],
        out_shape=[jax.ShapeDtypeStruct((T, RWKV_COLS), F32), bf(group_w), bf(group_w), bf(group_w),
                   bf(DIFF_QK_COLS), bf(Cc), jax.ShapeDtypeStruct((W, T), BF16)],
        scratch_shapes=[pltpu.VMEM((D_MODEL // LANES, tm, LANES), F32), pltpu.VMEM((tm + CONV_HALO, Cc), F32)],
        compiler_params=_params("arbitrary"),
        name="inproj",
    )(x2, gain, w_mix, w_vt, dw_w, dw_b, ln_g, ln_b)


_MASK_STRICT, _MASK_INCL, _MASK_EYE, _MASK_LEVEL0 = 0, 1, 2, 3
_INV_LEVELS = (1, 2, 4, 8, 16, 32)


def _rwkv_masks():
    n = RWKV_HEADS * RWKV_CHUNK
    r = np.arange(n)[:, None]
    c = np.arange(n)[None, :]
    same = (r // RWKV_CHUNK) == (c // RWKV_CHUNK)
    out = [same & (r > c), same & (r >= c), r == c]
    for s in _INV_LEVELS:
        out.append((r // (2 * s) == c // (2 * s)) & (r % (2 * s) >= s) & (c % (2 * s) < s))
    head = (r // RWKV_CHUNK) == (c // HEAD_DIM)
    out.append(head)
    return np.stack(out).astype(np.float32)


_MASK_HEAD = 3 + len(_INV_LEVELS)


def _chunk_sum_mats(ts):
    r = np.arange(ts)[:, None]
    c = np.arange(ts)[None, :]
    same = (r // RWKV_CHUNK) == (c // RWKV_CHUNK)
    return np.stack([same & (c <= r), same]).astype(np.float32)


def _head_sum_mat(width, group):
    r = np.arange(width)[:, None]
    c = np.arange(width)[None, :]
    return ((r // group) == (c // group)).astype(np.float32)


def _rwkv_kernel(p_ref, mu_ref, w0_ref, a0_ref, wa_ref, gup_ref, kk_ref, ka_ref, rk_ref, lng_ref, lnb_ref,
                 masks_ref, csum_ref, hsum_ref, o_ref,
                 state_ref, prev_ref, at_ref, bt_ref, kt_ref, rt_ref, bh_ref, kh_ref, v_ref, ec_ref, oc_ref,
                 lhs_ref, upd_ref, vr_ref, arb_ref, ut_ref, arkv_ref):
    nseq, ts = p_ref.shape[0], p_ref.shape[1]
    W = RWKV_WIDTH
    C = RWKV_CHUNK
    n = RWKV_HEADS * C
    nc = ts // C

    @pl.when(pl.program_id(1) == 0)
    def _():
        state_ref[...] = jnp.zeros_like(state_ref)
        prev_ref[...] = jnp.zeros_like(prev_ref)

    hsum = hsum_ref[...]
    kept = []
    for q in range(nseq):
        p = p_ref[q]
        row = lax.broadcasted_iota(jnp.int32, p.shape, 0)
        p_prev = jnp.where(row == 0, prev_ref[q], pltpu.roll(p, 1, 0))
        prev_ref[q] = p[ts - 1:ts, :]
        p = p + (p_prev - p) * mu_ref[...]

        r = p[:, 0:W]
        k = p[:, W:2 * W]
        v = p[:, 2 * W:3 * W]
        zwa = p[:, 3 * W:3 * W + RWKV_LORA_COLS]
        zg = p[:, 3 * W + RWKV_LORA_COLS:]
        lane = lax.broadcasted_iota(jnp.int32, zwa.shape, 1)
        tz = jnp.where(lane < RWKV_DECAY_RANK, jnp.tanh(zwa), zwa)
        wa = _dot(tz.astype(BF16), wa_ref[...])
        logw = -RWKV_DECAY_SCALE * _sigmoid(w0_ref[...] + wa[:, :W])
        alpha = _sigmoid(a0_ref[...] + wa[:, W:])
        g = _dot(_sigmoid(zg).astype(BF16), gup_ref[...])

        kk = k * kk_ref[...]
        kk = kk * lax.rsqrt(jnp.maximum(_dot_split(kk * kk, hsum), 1e-24))
        k2 = k * (1.0 + (alpha - 1.0) * ka_ref[...])
        bvec = kk * alpha

        cum = _dot_split_rhs(csum_ref[0], logw)
        cum_end = _dot_split_rhs(csum_ref[1], logw)
        e_neg = jnp.exp(-cum)
        e_rem = jnp.exp(cum_end - cum)
        at_ref[q] = -kk * jnp.exp(cum - logw)
        bt_ref[q] = bvec * e_neg
        kt_ref[q] = k2 * e_neg
        rt_ref[q] = r * jnp.exp(cum)
        bh_ref[q] = bvec * e_rem
        kh_ref[q] = k2 * e_rem
        v_ref[q] = v
        ec_ref[q] = jnp.exp(cum_end)
        kept.append((r * k2 * rk_ref[...], v, g))

    def tile4(x):
        x4 = jnp.concatenate([x, x, x, x], axis=0)
        return jnp.where(masks_ref[_MASK_HEAD] > 0, x4, 0.0).astype(BF16)

    def keep(m, x):
        return jnp.where(masks_ref[m] > 0, x, 0.0)

    chains = [(q, c) for q in range(nseq) for c in range(nc)]
    rows = [slice(c * C, (c + 1) * C) for _, c in chains]
    at = [tile4(at_ref[q, r, :]) for (q, _), r in zip(chains, rows)]
    vr = [tile4(v_ref[q, r, :]) for (q, _), r in zip(chains, rows)]
    a_ab, akv = [], []
    for j, ((q, c), r) in enumerate(zip(chains, rows)):
        rt = tile4(rt_ref[q, r, :])
        lhs = jnp.concatenate([at[j], rt], axis=0)
        rhs = jnp.concatenate([tile4(bt_ref[q, r, :]), tile4(kt_ref[q, r, :])], axis=0)
        a = _dot_nt(lhs, rhs)
        a_ab.append(keep(_MASK_STRICT, a[:n, :n]))
        arb_ref[q, c] = keep(_MASK_INCL, a[n:, :n]).astype(BF16)
        akv_arkv = _dot(jnp.concatenate([keep(_MASK_STRICT, a[:n, n:]), keep(_MASK_INCL, a[n:, n:])],
                                        axis=0).astype(BF16), vr[j])
        akv.append(akv_arkv[:n].astype(BF16))
        arkv_ref[q, c] = akv_arkv[n:]
        lhs_ref[q, c, n:2 * n, :] = rt
        vr_ref[q, c] = vr[j]
        upd_ref[q, c, 0:n, :] = tile4(bh_ref[q, r, :])
        upd_ref[q, c, n:2 * n, :] = tile4(kh_ref[q, r, :])

    def lower_rows(x, s):
        return jnp.concatenate([x[k * 2 * s + s:(k + 1) * 2 * s] for k in range(n // (2 * s))], axis=0)

    def scatter_lower(y, s):
        zero = jnp.zeros((s, y.shape[1]), y.dtype)
        return jnp.concatenate([p for k in range(n // (2 * s)) for p in (zero, y[k * s:(k + 1) * s])], axis=0)

    t = [masks_ref[_MASK_EYE] + keep(_MASK_LEVEL0, a) for a in a_ab]
    for li in range(1, len(_INV_LEVELS)):
        s = _INV_LEVELS[li]
        tb = [x.astype(BF16) for x in t]
        x_lvl = [keep(_MASK_LEVEL0 + li, a) for a in a_ab]
        if s % 8 == 0:
            xt = [scatter_lower(_dot(lower_rows(x, s).astype(BF16), y), s).astype(BF16) for x, y in zip(x_lvl, tb)]
            t = [x + scatter_lower(_dot(lower_rows(x, s).astype(BF16), z), s) for x, z in zip(t, xt)]
        else:
            xt = [_dot(x.astype(BF16), y).astype(BF16) for x, y in zip(x_lvl, tb)]
            t = [x + _dot(y, z) for x, y, z in zip(t, tb, xt)]
    for j, (q, c) in enumerate(chains):
        tb = t[j].astype(BF16)
        lhs_ref[q, c, 0:n, :] = _dot(tb, at[j]).astype(BF16)
        ut_ref[q, c] = _dot(tb, akv[j])

    for c in range(nc):
        s = [state_ref[q] for q in range(nseq)]
        g_us = [_dot_nt(lhs_ref[q, c], s[q].astype(BF16)) for q in range(nseq)]
        ub = [(g_us[q][:n] + ut_ref[q, c]).astype(BF16) for q in range(nseq)]
        o = [g_us[q][n:] + _dot(arb_ref[q, c], ub[q]) + arkv_ref[q, c] for q in range(nseq)]
        upd = [_dot_tn(jnp.concatenate([ub[q], vr_ref[q, c]], axis=0), upd_ref[q, c]) for q in range(nseq)]
        for q in range(nseq):
            oc_ref[q, c * C:(c + 1) * C, :] = o[q][0:C] + o[q][C:2 * C] + o[q][2 * C:3 * C] + o[q][3 * C:4 * C]
            state_ref[q] = s[q] * ec_ref[q, c * C:c * C + 1, :] + upd[q]

    hmean = hsum * (1.0 / HEAD_DIM)
    for q, (rk2, v, g) in enumerate(kept):
        o = oc_ref[q]
        mean = _dot_split(o, hmean)
        d = o - mean
        var = _dot_split(d * d, hmean)
        o = d * lax.rsqrt(var + RWKV_LN_EPS) * lng_ref[...] + lnb_ref[...]
        bonus = _dot_split(rk2, hsum)
        o_ref[q] = ((o + bonus * v) * g).astype(o_ref.dtype)


def _rwkv(p, mu, w0, a0, wa_up, g_up, k_k, k_a, r_k, ln_g, ln_b):
    B, S, _ = p.shape
    ts = RWKV_TILE
    nseq = RWKV_SEQS
    assert B % nseq == 0 and S % ts == 0
    W = RWKV_WIDTH
    n = RWKV_HEADS * RWKV_CHUNK
    nc = ts // RWKV_CHUNK
    masks = jnp.asarray(_rwkv_masks())
    csum = jnp.asarray(_chunk_sum_mats(ts), BF16)
    hsum = jnp.asarray(_head_sum_mat(W, HEAD_DIM), BF16)
    vec = _const_spec((1, W))
    buf = lambda: pltpu.VMEM((nseq, ts, W), F32)
    return pl.pallas_call(
        _rwkv_kernel,
        grid=(B // nseq, S // ts),
        in_specs=[pl.BlockSpec((nseq, ts, RWKV_COLS), lambda b, s: (b, s, 0)),
                  _const_spec((1, RWKV_COLS)), vec, vec, _const_spec((RWKV_LORA_COLS, 2 * W)),
                  _const_spec((RWKV_GATE_RANK, W)),
                  vec, vec, vec, vec, vec,
                  _const_spec(masks.shape), _const_spec(csum.shape), _const_spec(hsum.shape)],
        out_specs=pl.BlockSpec((nseq, ts, W), lambda b, s: (b, s, 0)),
        out_shape=jax.ShapeDtypeStruct((B, S, W), BF16),
        scratch_shapes=[pltpu.VMEM((nseq, n, n), F32), pltpu.VMEM((nseq, 1, RWKV_COLS), F32)]
                       + [buf() for _ in range(9)]
                       + [pltpu.VMEM((nseq, nc, 2 * n, n), BF16), pltpu.VMEM((nseq, nc, 2 * n, n), BF16),
                          pltpu.VMEM((nseq, nc, n, n), BF16), pltpu.VMEM((nseq, nc, n, n), BF16),
                          pltpu.VMEM((nseq, nc, n, n), F32), pltpu.VMEM((nseq, nc, n, n), F32)],
        compiler_params=_params("parallel", "arbitrary"),
        name="rwkv7",
    )(p, mu, w0, a0, wa_up, g_up, k_k, k_a, r_k, ln_g, ln_b, masks, csum, hsum)


def _alibi_slopes(n):
    return 2.0 ** (-8.0 * np.arange(1, n + 1) / n)


def _dil_bias(group):
    window, dilation = DIL_PATTERNS[group]
    n = window // dilation
    slopes = _alibi_slopes(DIL_HEADS).reshape(DIL_GROUPS, -1)[group]
    rel = n + np.arange(n)[:, None] - np.arange(2 * n)[None, :]
    valid = (rel >= 0) & (rel <= n)
    bias = -slopes[:, None, None] * (dilation * rel).astype(np.float64)[None]
    return np.where(valid[None], bias, NEG_INF).astype(np.float32)


def _dil_attend(blocks, bias_of, front):
    shape = blocks[0][0].shape
    heads = shape[1] // HEAD_DIM
    head = lax.broadcasted_iota(jnp.int32, shape, 1) // HEAD_DIM
    pairs = [(b, h) for b in range(len(blocks)) for h in range(heads)]
    s = [_dot_nt(jnp.where(head == h, blocks[b][0], jnp.zeros(shape, BF16)), blocks[b][1]) + bias_of(h)
         for b, h in pairs]
    if front is not None:
        s = [jnp.where(front, NEG_INF, x) for x in s]
    m = [jnp.max(x, axis=-1, keepdims=True) for x in s]
    e = [jnp.exp2(x - mx) for x, mx in zip(s, m)]
    l = [jnp.sum(x, axis=-1, keepdims=True) for x in e]
    per_half = LANES // HEAD_DIM
    pv = [_dot(x.astype(BF16), blocks[b][2][:, (h // per_half) * LANES:(h // per_half + 1) * LANES])
          for x, (b, h) in zip(e, pairs)]
    low = lax.broadcasted_iota(jnp.int32, (shape[0], LANES), 1) < HEAD_DIM
    results = []
    for b in range(len(blocks)):
        outs, lses = [], []
        for half in range(heads // per_half):
            i0, i1 = b * heads + half * per_half, b * heads + half * per_half + 1
            outs.append(jnp.where(low, pv[i0] / l[i0], pv[i1] / l[i1]))
            lses.append(jnp.where(low, m[i0] + jnp.log(l[i0]) * _LOG2E, m[i1] + jnp.log(l[i1]) * _LOG2E))
        results.append((outs, lses))
    return results


def _dil_kernel(c0_ref, c1_ref, c2_ref, p0_ref, p1_ref, p2_ref, b0_ref, b1_ref, b2_ref, y_ref, og_ref, lg_ref):
    n = DIL_BLOCK
    W = BRANCH_WIDTH
    sub = DIL_SUB
    first = pl.program_id(1) == 0
    front = jnp.logical_and(first, lax.broadcasted_iota(jnp.int32, (n, 2 * n), 1) < n)
    qscale = HEAD_DIM ** -0.5 * _LOG2E
    kcols, vcols, qcols = slice(0, W), slice(W, 2 * W), slice(2 * W, 3 * W)

    def scaled(q):
        return (q.astype(F32) * qscale).astype(BF16)

    def put(g, rows, out, lse):
        for half in range(W // LANES):
            og_ref[g, half, rows, :] = out[half]
            lg_ref[g, half, rows, :] = lse[half]

    def block_of(prev, cur):
        return (scaled(cur[:, qcols]), jnp.concatenate([prev[:, kcols], cur[:, kcols]], axis=0),
                jnp.concatenate([prev[:, vcols], cur[:, vcols]], axis=0))

    (out, lse), = _dil_attend([block_of(p0_ref[0], c0_ref[0, 0:n, :])], lambda h: b0_ref[h], front)
    put(0, pl.ds(0, n), out, lse)
    per0 = DIL_G0_BLOCKS_PER_TRIP

    def g0_body(i, carry):
        blocks, rows = [], []
        for j in range(per0):
            r0 = pl.multiple_of((i * per0 + j) * n, n)
            kv = c0_ref[0, pl.ds(r0, 2 * n), 0:2 * W]
            blocks.append((scaled(c0_ref[0, pl.ds(r0 + n, n), qcols]), kv[:, kcols], kv[:, vcols]))
            rows.append(pl.ds(r0 + n, n))
        for r, (out, lse) in zip(rows, _dil_attend(blocks, lambda h: b0_ref[h], None)):
            put(0, r, out, lse)
        return carry

    assert (DIL_TILE // n - 1) % per0 == 0
    lax.fori_loop(0, (DIL_TILE // n - 1) // per0, g0_body, 0)

    d1 = DIL_PATTERNS[1][1]

    def g1_rows(u, r):
        return pl.ds(pl.multiple_of(u * sub + r * n, n), n)

    def g1_run(u, prevs, front_mask):
        blocks = [block_of(prevs[r], c1_ref[0, g1_rows(u, r), :]) for r in range(d1)]
        for r, (out, lse) in enumerate(_dil_attend(blocks, lambda h: b1_ref[h], front_mask)):
            put(1, pl.ds(u * sub + r, n, stride=d1), out, lse)

    g1_run(0, [p1_ref[0, r * n:(r + 1) * n, :] for r in range(d1)], front)

    def g1_body(u, carry):
        g1_run(u, [c1_ref[0, g1_rows(u - 1, r), 0:2 * W] for r in range(d1)], None)
        return carry

    lax.fori_loop(1, DIL_TILE // sub, g1_body, 0)

    d2 = DIL_PATTERNS[2][1]
    piece = sub // d2
    per2 = DIL_G2_BLOCKS_PER_TRIP

    def g2_body(i, carry):
        def rows_of(ref, r, cols):
            return jnp.concatenate(
                [ref[0, pl.ds(pl.multiple_of(u * sub + r * piece, piece), piece), cols]
                 for u in range(DIL_TILE // sub)], axis=0)
        rs = [i * per2 + j for j in range(per2)]
        blocks = [block_of(rows_of(p2_ref, r, slice(0, 2 * W)), rows_of(c2_ref, r, slice(0, 3 * W))) for r in rs]
        for r, (out, lse) in zip(rs, _dil_attend(blocks, lambda h: b2_ref[h], front)):
            put(2, pl.ds(r, n, stride=d2), out, lse)
        return carry

    assert d2 % per2 == 0
    lax.fori_loop(0, d2 // per2, g2_body, 0)

    rows_per = 256

    def combine(j, carry):
        rows = pl.ds(pl.multiple_of(j * rows_per, rows_per), rows_per)
        for half in range(W // LANES):
            ls = [lg_ref[g, half, rows, :] for g in range(DIL_GROUPS)]
            m = jnp.maximum(jnp.maximum(ls[0], ls[1]), ls[2])
            es = [jnp.exp2(x - m) for x in ls]
            num = es[0] * og_ref[0, half, rows, :] + es[1] * og_ref[1, half, rows, :] + es[2] * og_ref[2, half, rows, :]
            y_ref[0, rows, half * LANES:(half + 1) * LANES] = (num / (es[0] + es[1] + es[2])).astype(y_ref.dtype)
        return carry

    lax.fori_loop(0, DIL_TILE // rows_per, combine, 0)


def _dilated(pd0, pd1, pd2):
    B, S, _ = pd0.shape
    W = BRANCH_WIDTH
    T = DIL_TILE
    assert S % T == 0
    for (window, d) in DIL_PATTERNS:
        assert window // d == DIL_BLOCK
    assert DIL_PATTERNS[0][1] == 1 and DIL_PATTERNS[1][1] * DIL_BLOCK == DIL_SUB and DIL_PATTERNS[2][1] * DIL_BLOCK == T
    biases = [jnp.asarray(_dil_bias(g) * _LOG2E) for g in range(DIL_GROUPS)]
    cur = pl.BlockSpec((1, T, 3 * W), lambda b, i: (b, i, 0))
    prev = lambda rows: pl.BlockSpec((1, rows, 2 * W), lambda b, i: (b, jnp.maximum(i * (T // rows) - 1, 0), 0))
    return pl.pallas_call(
        _dil_kernel,
        grid=(B, S // T),
        in_specs=[cur, cur, cur, prev(DIL_BLOCK), prev(DIL_SUB), prev(T)] + [_const_spec(b.shape) for b in biases],
        out_specs=pl.BlockSpec((1, T, W), lambda b, i: (b, i, 0)),
        out_shape=jax.ShapeDtypeStruct((B, S, W), BF16),
        scratch_shapes=[pltpu.VMEM((DIL_GROUPS, W // LANES, T, LANES), F32)] * 2,
        compiler_params=_params("parallel", "arbitrary"),
        name="dilated",
    )(pd0, pd1, pd2, pd0, pd1, pd2, *biases)


def _diff_bias():
    t = DIFF_TILE
    slopes = _alibi_slopes(DIFF_HEADS) * _LOG2E
    rel = (np.arange(t)[None, :] - np.arange(t)[:, None]).astype(np.float64)
    off = -slopes[:, None, None] * rel[None]
    diag = np.where(rel[None] >= 0, off, NEG_INF)
    return off.astype(np.float32), diag.astype(np.float32)


def _diff_kernel(q_ref, k_ref, vt_ref, boff_ref, bdiag_ref, lamv_ref, gain_ref, hsum_ref, o_ref, m_ref, acc_ref,
                 s_ref, qt_ref, *, lambda_init):
    t = DIFF_TILE
    n_maps = 2 * DIFF_HEADS
    ones_rows = DIFF_ACC_ROWS - HEAD_DIM
    qi = pl.program_id(1)
    q = q_ref[0].astype(F32) * (DIFF_QK_DIM ** -0.5 * _LOG2E)
    maps_per_half = LANES // DIFF_QK_DIM
    lane = lax.broadcasted_iota(jnp.int32, (t, LANES), 1)
    for c in range(n_maps):
        half = c // maps_per_half
        qh = jnp.where(lane // DIFF_QK_DIM == c % maps_per_half, q[:, half * LANES:(half + 1) * LANES], 0.0)
        qt_ref[c] = qh.T.astype(BF16)
    slopes = _alibi_slopes(DIFF_HEADS) * _LOG2E

    def scores(k, c):
        half = c // maps_per_half
        return _dot(k[:, half * LANES:(half + 1) * LANES], qt_ref[c])

    m_ref[...] = jnp.full(m_ref.shape, NEG_INF, F32)
    acc_ref[...] = jnp.zeros_like(acc_ref)

    def keys(kb):
        return pl.ds(pl.multiple_of(kb * t, t), t)

    def fold(kb, slot, bias_ref, k_next):
        ones = jnp.ones((ones_rows, t), BF16)
        vt = [jnp.concatenate([vt_ref[h * HEAD_DIM:(h + 1) * HEAD_DIM, keys(kb)], ones], axis=0)
              for h in range(DIFF_HEADS)]
        shift = ((qi - kb) * t).astype(F32)
        for c in range(n_maps):
            if k_next is not None:
                s_ref[1 - slot, c] = scores(k_next, c)
            block_bias = float(slopes[c // 2]) * shift
            s = s_ref[slot, c] + bias_ref[c // 2]
            m_prev = m_ref[c, 0:1, :]
            m_new = jnp.maximum(m_prev, jnp.max(s, axis=0, keepdims=True) - block_bias)
            e = jnp.exp2(s - (m_new + block_bias)).astype(BF16)
            m_ref[c] = jnp.broadcast_to(m_new, m_ref.shape[1:])
            acc_ref[c] = acc_ref[c] * jnp.exp2(m_prev - m_new) + _dot(vt[c // 2], e)

    k0 = k_ref[0, keys(0), :]
    for c in range(n_maps):
        s_ref[0, c] = scores(k0, c)

    def pair(jj, carry):
        fold(2 * jj, 0, boff_ref, k_ref[0, keys(2 * jj + 1), :])
        fold(2 * jj + 1, 1, boff_ref, k_ref[0, keys(2 * jj + 2), :])
        return carry

    lax.fori_loop(0, qi // 2, pair, 0)

    @pl.when(qi % 2 == 1)
    def _():
        fold(qi - 1, 0, boff_ref, k_ref[0, keys(qi), :])
        fold(qi, 1, bdiag_ref, None)

    @pl.when(qi % 2 == 0)
    def _():
        fold(qi, 0, bdiag_ref, None)

    lamv = lamv_ref[...]
    lam = (jnp.exp(jnp.sum(lamv[0:1] * lamv[1:2], axis=-1, keepdims=True))
           - jnp.exp(jnp.sum(lamv[2:3] * lamv[3:4], axis=-1, keepdims=True)) + lambda_init)
    heads = []
    for h in range(DIFF_HEADS):
        a0 = acc_ref[2 * h]
        a1 = acc_ref[2 * h + 1]
        heads.append(a0[0:HEAD_DIM] / a0[HEAD_DIM:HEAD_DIM + 1]
                     - lam * (a1[0:HEAD_DIM] / a1[HEAD_DIM:HEAD_DIM + 1]))
    o = jnp.concatenate(heads, axis=0).T
    ms = _dot_split(o * o, hsum_ref[...]) * (1.0 / HEAD_DIM)
    o = o * lax.rsqrt(ms + DIFF_SUBLN_EPS) * gain_ref[...]
    o_ref[0] = (o * (1.0 - lambda_init)).astype(o_ref.dtype)


def _diff_attention(p_qk, p_vt, lamv, subln_g, lambda_init):
    B, S, _ = p_qk.shape
    t = DIFF_TILE
    assert S % t == 0
    W = BRANCH_WIDTH
    boff, bdiag = (jnp.asarray(b) for b in _diff_bias())
    hsum = jnp.asarray(_head_sum_mat(W, HEAD_DIM), BF16)
    gain = jnp.tile(subln_g.reshape(1, HEAD_DIM), (1, DIFF_HEADS))
    single = lambda shape: pl.BlockSpec(shape, lambda b, i: (0,) * len(shape), pipeline_mode=pl.Buffered(1))
    return pl.pallas_call(
        functools.partial(_diff_kernel, lambda_init=lambda_init),
        grid=(B, S // t),
        in_specs=[pl.BlockSpec((1, t, W), lambda b, i: (b, i, 0)), pl.BlockSpec((1, S, W), lambda b, i: (b, 0, 1)),
                  pl.BlockSpec((W, S), lambda b, i: (0, b)),
                  single(boff.shape), single(bdiag.shape), _const_spec(lamv.shape), _const_spec((1, W)),
                  _const_spec((W, W))],
        out_specs=pl.BlockSpec((1, t, W), lambda b, i: (b, i, 0)),
        out_shape=jax.ShapeDtypeStruct((B, S, W), BF16),
        scratch_shapes=[pltpu.VMEM((2 * DIFF_HEADS, 8, t), F32), pltpu.VMEM((2 * DIFF_HEADS, DIFF_ACC_ROWS, t), F32),
                        pltpu.VMEM((2, 2 * DIFF_HEADS, t, t), F32), pltpu.VMEM((2 * DIFF_HEADS, LANES, t), BF16)],
        compiler_params=_params("parallel", "arbitrary"),
        name="diffattn",
    )(p_qk, p_qk, p_vt, boff, bdiag, lamv, gain, hsum)


def _conformer_tile(p, first, u_ref, w_ref, b_ref, g_ref, beta_ref, o_ref):
    ts = p.shape[0]
    Cc = CONV_CHANNELS
    halo = CONV_HALO
    rows = CONV_ROWS

    @pl.when(first)
    def _():
        u_ref[0:halo, :] = jnp.zeros((halo, Cc), F32)

    @pl.when(jnp.logical_not(first))
    def _():
        u_ref[0:halo, :] = u_ref[ts:ts + halo, :]

    u_ref[halo:halo + ts, :] = p[:, :Cc] * _sigmoid(p[:, Cc:])
    first_tap = halo - CONV_WIDTH + 1
    for r0 in range(0, ts, rows):
        halves = []
        for lanes in (slice(0, LANES), slice(LANES, Cc)):
            acc = jnp.zeros((rows, LANES), F32) + b_ref[:, lanes]
            win = u_ref[r0:r0 + rows + halo, lanes]
            for b in range(8):
                wb = win if b == 0 else pltpu.roll(win, rows + halo - b, 0)
                for a in range(halo // 8 + 1):
                    j = 8 * a + b - first_tap
                    if 0 <= j < CONV_WIDTH:
                        acc = acc + w_ref[j:j + 1, lanes] * wb[8 * a:8 * a + rows]
            halves.append(acc)
        acc = jnp.concatenate(halves, axis=1)
        mu = jnp.mean(acc, axis=-1, keepdims=True)
        d = acc - mu
        var = jnp.mean(d * d, axis=-1, keepdims=True)
        y = d * lax.rsqrt(var + CONV_LN_EPS) * g_ref[...] + beta_ref[...]
        o_ref[r0:r0 + rows, :] = (y * _sigmoid(y)).astype(o_ref.dtype)


def _merge_kernel(x_ref, ya_ref, yb_ref, yc_ref, yd_ref,
                  gpre_ref, gpost_ref, wg_ref, gb_ref, wb_ref, wo_ref, out_ref):
    x = x_ref[...]
    h = _rms(x, gpre_ref[...]).astype(BF16)
    ys = (ya_ref[...], yb_ref[...], yc_ref[...], yd_ref[...])
    merged = None
    for n in range(N_BRANCHES):
        gate = _sigmoid(_dot(h, wg_ref[:, n * D_MODEL:(n + 1) * D_MODEL]) + gb_ref[n])
        term = gate * _dot(ys[n], wb_ref[n])
        merged = term if merged is None else merged + term
    y = _dot(merged.astype(BF16), wo_ref[...])
    out_ref[...] = x + _rms(y, gpost_ref[...])


def _merge(x2, ya, yb, yc, yd, gpre, gpost, w_gate, gate_bias, w_branch, w_out):
    T = x2.shape[0]
    tm = TOKEN_TILE
    D = D_MODEL
    W = BRANCH_WIDTH
    row = lambda w: pl.BlockSpec((tm, w), lambda i: (i, 0))
    single = lambda shape: pl.BlockSpec(shape, lambda i: (0,) * len(shape), pipeline_mode=pl.Buffered(1))
    return pl.pallas_call(
        _merge_kernel,
        grid=(T // tm,),
        in_specs=[row(D)] + [row(W)] * 4 + [_const_spec((1, D)), _const_spec((1, D)),
                  single((D, N_BRANCHES * D)), _const_spec((N_BRANCHES, 1, D)), single((N_BRANCHES, W, D)),
                  single((D, D))],
        out_specs=row(D),
        out_shape=jax.ShapeDtypeStruct((T, D), F32),
        compiler_params=_params("parallel"),
        name="merge",
    )(x2, ya, yb, yc, yd, gpre, gpost, w_gate, gate_bias, w_branch, w_out)


def _ffn_kernel(x_ref, gpre_ref, gpost_ref, wg_ref, wu_ref, wd_ref, out_ref):
    x = x_ref[...]
    h = _rms(x, gpre_ref[...]).astype(BF16)
    g = _dot(h, wg_ref[...])
    u = _dot(h, wu_ref[...])
    a = (g * _sigmoid(g) * u).astype(BF16)
    f = _dot(a, wd_ref[...])
    out_ref[...] = x + _rms(f, gpost_ref[...])


def _ffn(x2, gpre, gpost, w_gate, w_up, w_down):
    T = x2.shape[0]
    tm = TOKEN_TILE
    D = D_MODEL
    row = pl.BlockSpec((tm, D), lambda i: (i, 0))
    single = lambda shape: pl.BlockSpec(shape, lambda i: (0,) * len(shape), pipeline_mode=pl.Buffered(1))
    return pl.pallas_call(
        _ffn_kernel,
        grid=(T // tm,),
        in_specs=[row, _const_spec((1, D)), _const_spec((1, D)), single((D, D_FF)), single((D, D_FF)),
                  single((D_FF, D))],
        out_specs=row,
        out_shape=jax.ShapeDtypeStruct((T, D), F32),
        compiler_params=_params("parallel"),
        name="swiglu",
    )(x2, gpre, gpost, w_gate, w_up, w_down)


def _layer(x2, B, S, l, prm):
    D = D_MODEL
    W = BRANCH_WIDTH
    row = lambda a: a.reshape(1, -1)
    w_in = prm["w_in"][l]
    w_dil = w_in[:, RWKV_COLS:RWKV_COLS + DIL_COLS].reshape(D, 3, DIL_GROUPS, W)
    w_dil = jnp.concatenate([w_dil[:, 1], w_dil[:, 2], w_dil[:, 0]], axis=-1).transpose(1, 0, 2)
    diff0 = RWKV_COLS + DIL_COLS
    w_mix = jnp.concatenate([w_in[:, :RWKV_COLS]] + [w_dil[g] for g in range(DIL_GROUPS)]
                            + [w_in[:, diff0:diff0 + DIFF_QK_COLS], w_in[:, diff0 + DIFF_COLS:MIX_COLS]],
                            axis=1).astype(BF16)
    w_vt = w_in[:, diff0 + DIFF_QK_COLS:diff0 + DIFF_COLS].T.astype(BF16)
    w_gate = w_in[:, MIX_COLS:].astype(BF16)

    p_rwkv, pd0, pd1, pd2, p_qk, yd, p_vt = _inproj(
        x2, S, row(prm["norm_mix_pre"][l]), w_mix, w_vt, prm["conv_dw_w"][l].reshape(CONV_WIDTH, CONV_CHANNELS),
        row(prm["conv_dw_b"][l]), row(prm["conv_ln_g"][l]), row(prm["conv_ln_b"][l]))

    zeros = jnp.zeros((64, W), F32)
    wa_up = jnp.concatenate([jnp.concatenate([prm["rwkv_w_up"][l], zeros], axis=1),
                             jnp.concatenate([zeros, prm["rwkv_a_up"][l]], axis=1)], axis=0).astype(BF16)
    ya = _rwkv(p_rwkv.reshape(B, S, RWKV_COLS), row(prm["rwkv_mu"][l]), row(prm["rwkv_w0"][l]),
               row(prm["rwkv_a0"][l]), wa_up, prm["rwkv_g_up"][l].astype(BF16), row(prm["rwkv_k_k"][l]),
               row(prm["rwkv_k_a"][l]), row(prm["rwkv_r_k"][l]), row(prm["rwkv_ln_g"][l]),
               row(prm["rwkv_ln_b"][l]))

    group_w = DIL_COLS // DIL_GROUPS
    yb = _dilated(pd0.reshape(B, S, group_w), pd1.reshape(B, S, group_w), pd2.reshape(B, S, group_w))

    lambda_init = 0.8 - 0.6 * math.exp(-0.3 * l)
    lamv = jnp.stack([prm["diff_lam_q1"][l], prm["diff_lam_k1"][l], prm["diff_lam_q2"][l], prm["diff_lam_k2"][l]])
    yc = _diff_attention(p_qk.reshape(B, S, DIFF_QK_COLS), p_vt, lamv, prm["diff_subln_g"][l], lambda_init)

    x2 = _merge(x2, ya.reshape(B * S, W), yb.reshape(B * S, W), yc.reshape(B * S, W), yd.reshape(B * S, W),
                row(prm["norm_mix_pre"][l]), row(prm["norm_mix_post"][l]), w_gate,
                prm["gate_bias"][l].reshape(N_BRANCHES, 1, D), prm["w_branch"][l].astype(BF16),
                prm["w_out"][l].astype(BF16))

    return _ffn(x2, row(prm["norm_ffn_pre"][l]), row(prm["norm_ffn_post"][l]), prm["ffn_w_gate"][l].astype(BF16),
                prm["ffn_w_up"][l].astype(BF16), prm["ffn_w_down"][l].astype(BF16))


def kernel(x, norm_mix_pre, norm_mix_post, norm_ffn_pre, norm_ffn_post, w_in, gate_bias, rwkv_mu, rwkv_w0, rwkv_w_up, rwkv_a0, rwkv_a_up, rwkv_g_up, rwkv_k_k, rwkv_k_a, rwkv_r_k, rwkv_ln_g, rwkv_ln_b, diff_lam_q1, diff_lam_k1, diff_lam_q2, diff_lam_k2, diff_subln_g, conv_dw_w, conv_dw_b, conv_ln_g, conv_ln_b, w_branch, w_out, ffn_w_gate, ffn_w_up, ffn_w_down):
    prm = dict(locals())
    B, S, D = x.shape
    x2 = x.reshape(B * S, D)
    for l in range(w_in.shape[0]):
        x2 = _layer(x2, B, S, l, prm)
    return x2.reshape(B, S, D)
```

```python
import functools
import math

import numpy as np
import jax
import jax.numpy as jnp
from jax import lax
from jax.experimental import pallas as pl
from jax.experimental.pallas import tpu as pltpu

F32 = jnp.float32
BF16 = jnp.bfloat16
LANES = 128

D_MODEL = 1024
HEAD_DIM = 64
BRANCH_WIDTH = 256
N_BRANCHES = 4

RWKV_HEADS = 4
RWKV_WIDTH = 256
RWKV_COLS = 1024
RWKV_DECAY_SCALE = 0.606531
RWKV_LN_EPS = 64e-5
RWKV_DECAY_RANK = 64
RWKV_ICL_RANK = 64
RWKV_GATE_RANK = 128
RWKV_LORA_COLS = RWKV_DECAY_RANK + RWKV_ICL_RANK
RWKV_CHUNK = 64
RWKV_TILE = 256
RWKV_SEQS = 2

DIL_PATTERNS = ((128, 1), (512, 4), (2048, 16))
DIL_GROUPS = 3
DIL_HEADS = 12
DIL_COLS = 2304
DIL_BLOCK = 128
DIL_TILE = 2048
DIL_SUB = 512
DIL_G0_BLOCKS_PER_TRIP = 5
DIL_G2_BLOCKS_PER_TRIP = 8

DIFF_HEADS = 4
DIFF_QK_DIM = 32
DIFF_COLS = 768
DIFF_TILE = 512
DIFF_QK_COLS = 512
DIFF_ACC_ROWS = HEAD_DIM + 16
DIFF_SUBLN_EPS = 1e-5

CONV_CHANNELS = 256
CONV_WIDTH = 31
CONV_COLS = 512
CONV_LN_EPS = 1e-5
CONV_HALO = 32
CONV_ROWS = 64

MIX_COLS = RWKV_COLS + DIL_COLS + DIFF_COLS + CONV_COLS
D_FF = 2816
NORM_EPS = 1e-6
NEG_INF = -1e30
_LOG2E = 1.4426950408889634

VMEM_LIMIT_BYTES = 56 * 1024 * 1024
TOKEN_TILE = 512
ROW_GROUPS = 2


def _dot(a, b):
    return jnp.dot(a, b, preferred_element_type=F32)


def _dot_nt(a, b):
    return lax.dot_general(a, b, (((1,), (1,)), ((), ())), preferred_element_type=F32)


def _dot_tn(a, b):
    return lax.dot_general(a, b, (((0,), (0,)), ((), ())), preferred_element_type=F32)


def _dot_split(x, m):
    hi = x.astype(BF16)
    lo = (x - hi.astype(F32)).astype(BF16)
    return _dot(hi, m) + _dot(lo, m)


def _dot_split_rhs(m, x):
    hi = x.astype(BF16)
    lo = (x - hi.astype(F32)).astype(BF16)
    return _dot(m, hi) + _dot(m, lo)


def _rms(x, gain):
    return x * lax.rsqrt(jnp.mean(x * x, axis=-1, keepdims=True) + NORM_EPS) * gain


def _sigmoid(x):
    return 1.0 / (1.0 + jnp.exp(-x))


def _const_spec(shape):
    nd = len(shape)
    return pl.BlockSpec(shape, lambda *_: (0,) * nd)


def _params(*sem):
    return pltpu.CompilerParams(dimension_semantics=sem, vmem_limit_bytes=VMEM_LIMIT_BYTES)


def _inproj_kernel(x_ref, g_ref, w_ref, wvt_ref, cw_ref, cb_ref, cg_ref, cbeta_ref,
                   o_rwkv, o_dil0, o_dil1, o_dil2, o_diff, o_conv, o_vt, h_ref, u_ref, *, tiles_per_seq):
    tm = x_ref.shape[0]
    hf = _rms(x_ref[...], g_ref[...])
    h = hf.astype(BF16)
    conv0 = w_ref.shape[1] - CONV_COLS
    _conformer_tile(_dot(h, w_ref[:, conv0:]), pl.program_id(0) % tiles_per_seq == 0, u_ref, cw_ref, cb_ref, cg_ref,
                    cbeta_ref, o_conv)
    slabs = D_MODEL // LANES
    for j in range(slabs):
        h_ref[j] = hf[:, j * LANES:(j + 1) * LANES]

    def permuted(d):
        cols = [jnp.concatenate([h_ref[j, pl.ds(r, tm // d, stride=d), :] for r in range(d)], axis=0)
                for j in range(slabs)]
        return jnp.concatenate(cols, axis=1).astype(BF16)

    group_w = DIL_COLS // DIL_GROUPS
    col = 0
    for o_ref, width, lhs in ((o_rwkv, RWKV_COLS, h), (o_dil0, group_w, h),
                              (o_dil1, group_w, permuted(DIL_PATTERNS[1][1])),
                              (o_dil2, group_w, permuted(DIL_PATTERNS[2][1])),
                              (o_diff, DIFF_QK_COLS, h)):
        o_ref[...] = _dot(lhs, w_ref[:, col:col + width]).astype(o_ref.dtype)
        col += width
    o_vt[...] = _dot_nt(wvt_ref[...], h).astype(o_vt.dtype)


def _inproj(x2, seq_len, gain, w_mix, w_vt, dw_w, dw_b, ln_g, ln_b):
    T = x2.shape[0]
    tm = TOKEN_TILE
    assert tm == DIL_SUB and seq_len % tm == 0
    group_w = DIL_COLS // DIL_GROUPS
    W = BRANCH_WIDTH
    Cc = CONV_CHANNELS
    row = lambda w: pl.BlockSpec((tm, w), lambda i: (i, 0))
    bf = lambda w: jax.ShapeDtypeStruct((T, w), BF16)
    vec = _const_spec((1, Cc))
    return pl.pallas_call(
        functools.partial(_inproj_kernel, tiles_per_seq=seq_len // tm),
        grid=(T // tm,),
        in_specs=[row(D_MODEL), _const_spec((1, D_MODEL)), _const_spec((D_MODEL, MIX_COLS - W)),
                  _const_spec((W, D_MODEL)), _const_spec((CONV_WIDTH, Cc)), vec, vec, vec],
        out_specs=[row(RWKV_COLS), row(group_w), row(group_w), row(group_w), row(DIFF_QK_COLS), row(Cc),
                   pl.BlockSpec((W, tm), lambda i: (0, i))],
        out_shape=[jax.ShapeDtypeStruct((T, RWKV_COLS), F32), bf(group_w), bf(group_w), bf(group_w),
                   bf(DIFF_QK_COLS), bf(Cc), jax.ShapeDtypeStruct((W, T), BF16)],
        scratch_shapes=[pltpu.VMEM((D_MODEL // LANES, tm, LANES), F32), pltpu.VMEM((tm + CONV_HALO, Cc), F32)],
        compiler_params=_params("arbitrary"),
        name="inproj",
    )(x2, gain, w_mix, w_vt, dw_w, dw_b, ln_g, ln_b)


_MASK_STRICT, _MASK_INCL, _MASK_EYE, _MASK_LEVEL0 = 0, 1, 2, 3
_INV_LEVELS = (1, 2, 4, 8, 16, 32)


def _rwkv_masks():
    n = RWKV_HEADS * RWKV_CHUNK
    r = np.arange(n)[:, None]
    c = np.arange(n)[None, :]
    same = (r // RWKV_CHUNK) == (c // RWKV_CHUNK)
    out = [same & (r > c), same & (r >= c), r == c]
    for s in _INV_LEVELS:
        out.append((r // (2 * s) == c // (2 * s)) & (r % (2 * s) >= s) & (c % (2 * s) < s))
    head = (r // RWKV_CHUNK) == (c // HEAD_DIM)
    out.append(head)
    return np.stack(out).astype(np.float32)


_MASK_HEAD = 3 + len(_INV_LEVELS)


def _chunk_sum_mats(ts):
    r = np.arange(ts)[:, None]
    c = np.arange(ts)[None, :]
    same = (r // RWKV_CHUNK) == (c // RWKV_CHUNK)
    return np.stack([same & (c <= r), same]).astype(np.float32)


def _head_sum_mat(width, group):
    r = np.arange(width)[:, None]
    c = np.arange(width)[None, :]
    return ((r // group) == (c // group)).astype(np.float32)


def _rwkv_kernel(p_ref, mu_ref, w0_ref, a0_ref, wa_ref, gup_ref, kk_ref, ka_ref, rk_ref, lng_ref, lnb_ref,
                 masks_ref, csum_ref, hsum_ref, o_ref,
                 state_ref, prev_ref, at_ref, bt_ref, kt_ref, rt_ref, bh_ref, kh_ref, v_ref, ec_ref, oc_ref,
                 lhs_ref, upd_ref, vr_ref, arb_ref, ut_ref, arkv_ref):
    nseq, ts = p_ref.shape[0], p_ref.shape[1]
    W = RWKV_WIDTH
    C = RWKV_CHUNK
    n = RWKV_HEADS * C
    nc = ts // C

    @pl.when(pl.program_id(1) == 0)
    def _():
        state_ref[...] = jnp.zeros_like(state_ref)
        prev_ref[...] = jnp.zeros_like(prev_ref)

    hsum = hsum_ref[...]
    kept = []
    for q in range(nseq):
        p = p_ref[q]
        row = lax.broadcasted_iota(jnp.int32, p.shape, 0)
        p_prev = jnp.where(row == 0, prev_ref[q], pltpu.roll(p, 1, 0))
        prev_ref[q] = p[ts - 1:ts, :]
        p = p + (p_prev - p) * mu_ref[...]

        r = p[:, 0:W]
        k = p[:, W:2 * W]
        v = p[:, 2 * W:3 * W]
        zwa = p[:, 3 * W:3 * W + RWKV_LORA_COLS]
        zg = p[:, 3 * W + RWKV_LORA_COLS:]
        lane = lax.broadcasted_iota(jnp.int32, zwa.shape, 1)
        tz = jnp.where(lane < RWKV_DECAY_RANK, jnp.tanh(zwa), zwa)
        wa = _dot(tz.astype(BF16), wa_ref[...])
        logw = -RWKV_DECAY_SCALE * _sigmoid(w0_ref[...] + wa[:, :W])
        alpha = _sigmoid(a0_ref[...] + wa[:, W:])
        g = _dot(_sigmoid(zg).astype(BF16), gup_ref[...])

        kk = k * kk_ref[...]
        kk = kk * lax.rsqrt(jnp.maximum(_dot_split(kk * kk, hsum), 1e-24))
        k2 = k * (1.0 + (alpha - 1.0) * ka_ref[...])
        bvec = kk * alpha

        cum = _dot_split_rhs(csum_ref[0], logw)
        cum_end = _dot_split_rhs(csum_ref[1], logw)
        e_neg = jnp.exp(-cum)
        e_rem = jnp.exp(cum_end - cum)
        at_ref[q] = -kk * jnp.exp(cum - logw)
        bt_ref[q] = bvec * e_neg
        kt_ref[q] = k2 * e_neg
        rt_ref[q] = r * jnp.exp(cum)
        bh_ref[q] = bvec * e_rem
        kh_ref[q] = k2 * e_rem
        v_ref[q] = v
        ec_ref[q] = jnp.exp(cum_end)
        kept.append((r * k2 * rk_ref[...], v, g))

    def tile4(x):
        x4 = jnp.concatenate([x, x, x, x], axis=0)
        return jnp.where(masks_ref[_MASK_HEAD] > 0, x4, 0.0).astype(BF16)

    def keep(m, x):
        return jnp.where(masks_ref[m] > 0, x, 0.0)

    chains = [(q, c) for q in range(nseq) for c in range(nc)]
    rows = [slice(c * C, (c + 1) * C) for _, c in chains]
    at = [tile4(at_ref[q, r, :]) for (q, _), r in zip(chains, rows)]
    vr = [tile4(v_ref[q, r, :]) for (q, _), r in zip(chains, rows)]
    a_ab, akv = [], []
    for j, ((q, c), r) in enumerate(zip(chains, rows)):
        rt = tile4(rt_ref[q, r, :])
        lhs = jnp.concatenate([at[j], rt], axis=0)
        rhs = jnp.concatenate([tile4(bt_ref[q, r, :]), tile4(kt_ref[q, r, :])], axis=0)
        a = _dot_nt(lhs, rhs)
        a_ab.append(keep(_MASK_STRICT, a[:n, :n]))
        arb_ref[q, c] = keep(_MASK_INCL, a[n:, :n]).astype(BF16)
        akv_arkv = _dot(jnp.concatenate([keep(_MASK_STRICT, a[:n, n:]), keep(_MASK_INCL, a[n:, n:])],
                                        axis=0).astype(BF16), vr[j])
        akv.append(akv_arkv[:n].astype(BF16))
        arkv_ref[q, c] = akv_arkv[n:]
        lhs_ref[q, c, n:2 * n, :] = rt
        vr_ref[q, c] = vr[j]
        upd_ref[q, c, 0:n, :] = tile4(bh_ref[q, r, :])
        upd_ref[q, c, n:2 * n, :] = tile4(kh_ref[q, r, :])

    def lower_rows(x, s):
        return jnp.concatenate([x[k * 2 * s + s:(k + 1) * 2 * s] for k in range(n // (2 * s))], axis=0)

    def scatter_lower(y, s):
        zero = jnp.zeros((s, y.shape[1]), y.dtype)
        return jnp.concatenate([p for k in range(n // (2 * s)) for p in (zero, y[k * s:(k + 1) * s])], axis=0)

    t = [masks_ref[_MASK_EYE] + keep(_MASK_LEVEL0, a) for a in a_ab]
    for li in range(1, len(_INV_LEVELS)):
        s = _INV_LEVELS[li]
        tb = [x.astype(BF16) for x in t]
        x_lvl = [keep(_MASK_LEVEL0 + li, a) for a in a_ab]
        if s % 8 == 0:
            xt = [scatter_lower(_dot(lower_rows(x, s).astype(BF16), y), s).astype(BF16) for x, y in zip(x_lvl, tb)]
            t = [x + scatter_lower(_dot(lower_rows(x, s).astype(BF16), z), s) for x, z in zip(t, xt)]
        else:
            xt = [_dot(x.astype(BF16), y).astype(BF16) for x, y in zip(x_lvl, tb)]
            t = [x + _dot(y, z) for x, y, z in zip(t, tb, xt)]
    for j, (q, c) in enumerate(chains):
        tb = t[j].astype(BF16)
        lhs_ref[q, c, 0:n, :] = _dot(tb, at[j]).astype(BF16)
        ut_ref[q, c] = _dot(tb, akv[j])

    for c in range(nc):
        s = [state_ref[q] for q in range(nseq)]
        g_us = [_dot_nt(lhs_ref[q, c], s[q].astype(BF16)) for q in range(nseq)]
        ub = [(g_us[q][:n] + ut_ref[q, c]).astype(BF16) for q in range(nseq)]
        o = [g_us[q][n:] + _dot(arb_ref[q, c], ub[q]) + arkv_ref[q, c] for q in range(nseq)]
        upd = [_dot_tn(jnp.concatenate([ub[q], vr_ref[q, c]], axis=0), upd_ref[q, c]) for q in range(nseq)]
        for q in range(nseq):
            oc_ref[q, c * C:(c + 1) * C, :] = o[q][0:C] + o[q][C:2 * C] + o[q][2 * C:3 * C] + o[q][3 * C:4 * C]
            state_ref[q] = s[q] * ec_ref[q, c * C:c * C + 1, :] + upd[q]

    hmean = hsum * (1.0 / HEAD_DIM)
    for q, (rk2, v, g) in enumerate(kept):
        o = oc_ref[q]
        mean = _dot_split(o, hmean)
        d = o - mean
        var = _dot_split(d * d, hmean)
        o = d * lax.rsqrt(var + RWKV_LN_EPS) * lng_ref[...] + lnb_ref[...]
        bonus = _dot_split(rk2, hsum)
        o_ref[q] = ((o + bonus * v) * g).astype(o_ref.dtype)


def _rwkv(p, mu, w0, a0, wa_up, g_up, k_k, k_a, r_k, ln_g, ln_b):
    B, S, _ = p.shape
    ts = RWKV_TILE
    nseq = RWKV_SEQS
    assert B % nseq == 0 and S % ts == 0
    W = RWKV_WIDTH
    n = RWKV_HEADS * RWKV_CHUNK
    nc = ts // RWKV_CHUNK
    masks = jnp.asarray(_rwkv_masks())
    csum = jnp.asarray(_chunk_sum_mats(ts), BF16)
    hsum = jnp.asarray(_head_sum_mat(W, HEAD_DIM), BF16)
    vec = _const_spec((1, W))
    buf = lambda: pltpu.VMEM((nseq, ts, W), F32)
    return pl.pallas_call(
        _rwkv_kernel,
        grid=(B // nseq, S // ts),
        in_specs=[pl.BlockSpec((nseq, ts, RWKV_COLS), lambda b, s: (b, s, 0)),
                  _const_spec((1, RWKV_COLS)), vec, vec, _const_spec((RWKV_LORA_COLS, 2 * W)),
                  _const_spec((RWKV_GATE_RANK, W)),
                  vec, vec, vec, vec, vec,
                  _const_spec(masks.shape), _const_spec(csum.shape), _const_spec(hsum.shape)],
        out_specs=pl.BlockSpec((nseq, ts, W), lambda b, s: (b, s, 0)),
        out_shape=jax.ShapeDtypeStruct((B, S, W), BF16),
        scratch_shapes=[pltpu.VMEM((nseq, n, n), F32), pltpu.VMEM((nseq, 1, RWKV_COLS), F32)]
                       + [buf() for _ in range(9)]
                       + [pltpu.VMEM((nseq, nc, 2 * n, n), BF16), pltpu.VMEM((nseq, nc, 2 * n, n), BF16),
                          pltpu.VMEM((nseq, nc, n, n), BF16), pltpu.VMEM((nseq, nc, n, n), BF16),
                          pltpu.VMEM((nseq, nc, n, n), F32), pltpu.VMEM((nseq, nc, n, n), F32)],
        compiler_params=_params("parallel", "arbitrary"),
        name="rwkv7",
    )(p, mu, w0, a0, wa_up, g_up, k_k, k_a, r_k, ln_g, ln_b, masks, csum, hsum)


def _alibi_slopes(n):
    return 2.0 ** (-8.0 * np.arange(1, n + 1) / n)


def _dil_bias(group):
    window, dilation = DIL_PATTERNS[group]
    n = window // dilation
    slopes = _alibi_slopes(DIL_HEADS).reshape(DIL_GROUPS, -1)[group]
    rel = n + np.arange(n)[:, None] - np.arange(2 * n)[None, :]
    valid = (rel >= 0) & (rel <= n)
    bias = -slopes[:, None, None] * (dilation * rel).astype(np.float64)[None]
    return np.where(valid[None], bias, NEG_INF).astype(np.float32)


def _dil_attend(blocks, bias_of, front):
    shape = blocks[0][0].shape
    heads = shape[1] // HEAD_DIM
    head = lax.broadcasted_iota(jnp.int32, shape, 1) // HEAD_DIM
    pairs = [(b, h) for b in range(len(blocks)) for h in range(heads)]
    s = [_dot_nt(jnp.where(head == h, blocks[b][0], jnp.zeros(shape, BF16)), blocks[b][1]) + bias_of(h)
         for b, h in pairs]
    if front is not None:
        s = [jnp.where(front, NEG_INF, x) for x in s]
    m = [jnp.max(x, axis=-1, keepdims=True) for x in s]
    e = [jnp.exp2(x - mx) for x, mx in zip(s, m)]
    l = [jnp.sum(x, axis=-1, keepdims=True) for x in e]
    per_half = LANES // HEAD_DIM
    pv = [_dot(x.astype(BF16), blocks[b][2][:, (h // per_half) * LANES:(h // per_half + 1) * LANES])
          for x, (b, h) in zip(e, pairs)]
    low = lax.broadcasted_iota(jnp.int32, (shape[0], LANES), 1) < HEAD_DIM
    results = []
    for b in range(len(blocks)):
        outs, lses = [], []
        for half in range(heads // per_half):
            i0, i1 = b * heads + half * per_half, b * heads + half * per_half + 1
            outs.append(jnp.where(low, pv[i0] / l[i0], pv[i1] / l[i1]))
            lses.append(jnp.where(low, m[i0] + jnp.log(l[i0]) * _LOG2E, m[i1] + jnp.log(l[i1]) * _LOG2E))
        results.append((outs, lses))
    return results


def _dil_kernel(c0_ref, c1_ref, c2_ref, p0_ref, p1_ref, p2_ref, b0_ref, b1_ref, b2_ref, y_ref, og_ref, lg_ref):
    n = DIL_BLOCK
    W = BRANCH_WIDTH
    sub = DIL_SUB
    first = pl.program_id(1) == 0
    front = jnp.logical_and(first, lax.broadcasted_iota(jnp.int32, (n, 2 * n), 1) < n)
    qscale = HEAD_DIM ** -0.5 * _LOG2E
    kcols, vcols, qcols = slice(0, W), slice(W, 2 * W), slice(2 * W, 3 * W)

    def scaled(q):
        return (q.astype(F32) * qscale).astype(BF16)

    def put(g, rows, out, lse):
        for half in range(W // LANES):
            og_ref[g, half, rows, :] = out[half]
            lg_ref[g, half, rows, :] = lse[half]

    def block_of(prev, cur):
        return (scaled(cur[:, qcols]), jnp.concatenate([prev[:, kcols], cur[:, kcols]], axis=0),
                jnp.concatenate([prev[:, vcols], cur[:, vcols]], axis=0))

    (out, lse), = _dil_attend([block_of(p0_ref[0], c0_ref[0, 0:n, :])], lambda h: b0_ref[h], front)
    put(0, pl.ds(0, n), out, lse)
    per0 = DIL_G0_BLOCKS_PER_TRIP

    def g0_body(i, carry):
        blocks, rows = [], []
        for j in range(per0):
            r0 = pl.multiple_of((i * per0 + j) * n, n)
            kv = c0_ref[0, pl.ds(r0, 2 * n), 0:2 * W]
            blocks.append((scaled(c0_ref[0, pl.ds(r0 + n, n), qcols]), kv[:, kcols], kv[:, vcols]))
            rows.append(pl.ds(r0 + n, n))
        for r, (out, lse) in zip(rows, _dil_attend(blocks, lambda h: b0_ref[h], None)):
            put(0, r, out, lse)
        return carry

    assert (DIL_TILE // n - 1) % per0 == 0
    lax.fori_loop(0, (DIL_TILE // n - 1) // per0, g0_body, 0)

    d1 = DIL_PATTERNS[1][1]

    def g1_rows(u, r):
        return pl.ds(pl.multiple_of(u * sub + r * n, n), n)

    def g1_run(u, prevs, front_mask):
        blocks = [block_of(prevs[r], c1_ref[0, g1_rows(u, r), :]) for r in range(d1)]
        for r, (out, lse) in enumerate(_dil_attend(blocks, lambda h: b1_ref[h], front_mask)):
            put(1, pl.ds(u * sub + r, n, stride=d1), out, lse)

    g1_run(0, [p1_ref[0, r * n:(r + 1) * n, :] for r in range(d1)], front)

    def g1_body(u, carry):
        g1_run(u, [c1_ref[0, g1_rows(u - 1, r), 0:2 * W] for r in range(d1)], None)
        return carry

    lax.fori_loop(1, DIL_TILE // sub, g1_body, 0)

    d2 = DIL_PATTERNS[2][1]
    piece = sub // d2
    per2 = DIL_G2_BLOCKS_PER_TRIP

    def g2_body(i, carry):
        def rows_of(ref, r, cols):
            return jnp.concatenate(
                [ref[0, pl.ds(pl.multiple_of(u * sub + r * piece, piece), piece), cols]
                 for u in range(DIL_TILE // sub)], axis=0)
        rs = [i * per2 + j for j in range(per2)]
        blocks = [block_of(rows_of(p2_ref, r, slice(0, 2 * W)), rows_of(c2_ref, r, slice(0, 3 * W))) for r in rs]
        for r, (out, lse) in zip(rs, _dil_attend(blocks, lambda h: b2_ref[h], front)):
            put(2, pl.ds(r, n, stride=d2), out, lse)
        return carry

    assert d2 % per2 == 0
    lax.fori_loop(0, d2 // per2, g2_body, 0)

    rows_per = 256

    def combine(j, carry):
        rows = pl.ds(pl.multiple_of(j * rows_per, rows_per), rows_per)
        for half in range(W // LANES):
            ls = [lg_ref[g, half, rows, :] for g in range(DIL_GROUPS)]
            m = jnp.maximum(jnp.maximum(ls[0], ls[1]), ls[2])
            es = [jnp.exp2(x - m) for x in ls]
            num = es[0] * og_ref[0, half, rows, :] + es[1] * og_ref[1, half, rows, :] + es[2] * og_ref[2, half, rows, :]
            y_ref[0, rows, half * LANES:(half + 1) * LANES] = (num / (es[0] + es[1] + es[2])).astype(y_ref.dtype)
        return carry

    lax.fori_loop(0, DIL_TILE // rows_per, combine, 0)


def _dilated(pd0, pd1, pd2):
    B, S, _ = pd0.shape
    W = BRANCH_WIDTH
    T = DIL_TILE
    assert S % T == 0
    for (window, d) in DIL_PATTERNS:
        assert window // d == DIL_BLOCK
    assert DIL_PATTERNS[0][1] == 1 and DIL_PATTERNS[1][1] * DIL_BLOCK == DIL_SUB and DIL_PATTERNS[2][1] * DIL_BLOCK == T
    biases = [jnp.asarray(_dil_bias(g) * _LOG2E) for g in range(DIL_GROUPS)]
    cur = pl.BlockSpec((1, T, 3 * W), lambda b, i: (b, i, 0))
    prev = lambda rows: pl.BlockSpec((1, rows, 2 * W), lambda b, i: (b, jnp.maximum(i * (T // rows) - 1, 0), 0))
    return pl.pallas_call(
        _dil_kernel,
        grid=(B, S // T),
        in_specs=[cur, cur, cur, prev(DIL_BLOCK), prev(DIL_SUB), prev(T)] + [_const_spec(b.shape) for b in biases],
        out_specs=pl.BlockSpec((1, T, W), lambda b, i: (b, i, 0)),
        out_shape=jax.ShapeDtypeStruct((B, S, W), BF16),
        scratch_shapes=[pltpu.VMEM((DIL_GROUPS, W // LANES, T, LANES), F32)] * 2,
        compiler_params=_params("parallel", "arbitrary"),
        name="dilated",
    )(pd0, pd1, pd2, pd0, pd1, pd2, *biases)


def _diff_bias():
    t = DIFF_TILE
    slopes = _alibi_slopes(DIFF_HEADS) * _LOG2E
    rel = (np.arange(t)[None, :] - np.arange(t)[:, None]).astype(np.float64)
    off = -slopes[:, None, None] * rel[None]
    diag = np.where(rel[None] >= 0, off, NEG_INF)
    return off.astype(np.float32), diag.astype(np.float32)


def _diff_kernel(q_ref, k_ref, vt_ref, boff_ref, bdiag_ref, lamv_ref, gain_ref, hsum_ref, o_ref, m_ref, acc_ref,
                 s_ref, qt_ref, *, lambda_init):
    t = DIFF_TILE
    n_maps = 2 * DIFF_HEADS
    ones_rows = DIFF_ACC_ROWS - HEAD_DIM
    qi = pl.program_id(1)
    q = q_ref[0].astype(F32) * (DIFF_QK_DIM ** -0.5 * _LOG2E)
    maps_per_half = LANES // DIFF_QK_DIM
    lane = lax.broadcasted_iota(jnp.int32, (t, LANES), 1)
    for c in range(n_maps):
        half = c // maps_per_half
        qh = jnp.where(lane // DIFF_QK_DIM == c % maps_per_half, q[:, half * LANES:(half + 1) * LANES], 0.0)
        qt_ref[c] = qh.T.astype(BF16)
    slopes = _alibi_slopes(DIFF_HEADS) * _LOG2E

    def scores(k, c):
        half = c // maps_per_half
        return _dot(k[:, half * LANES:(half + 1) * LANES], qt_ref[c])

    m_ref[...] = jnp.full(m_ref.shape, NEG_INF, F32)
    acc_ref[...] = jnp.zeros_like(acc_ref)

    def keys(kb):
        return pl.ds(pl.multiple_of(kb * t, t), t)

    def fold(kb, slot, bias_ref, k_next):
        ones = jnp.ones((ones_rows, t), BF16)
        vt = [jnp.concatenate([vt_ref[h * HEAD_DIM:(h + 1) * HEAD_DIM, keys(kb)], ones], axis=0)
              for h in range(DIFF_HEADS)]
        shift = ((qi - kb) * t).astype(F32)
        for c in range(n_maps):
            if k_next is not None:
                s_ref[1 - slot, c] = scores(k_next, c)
            block_bias = float(slopes[c // 2]) * shift
            s = s_ref[slot, c] + bias_ref[c // 2]
            m_prev = m_ref[c, 0:1, :]
            m_new = jnp.maximum(m_prev, jnp.max(s, axis=0, keepdims=True) - block_bias)
            e = jnp.exp2(s - (m_new + block_bias)).astype(BF16)
            m_ref[c] = jnp.broadcast_to(m_new, m_ref.shape[1:])
            acc_ref[c] = acc_ref[c] * jnp.exp2(m_prev - m_new) + _dot(vt[c // 2], e)

    k0 = k_ref[0, keys(0), :]
    for c in range(n_maps):
        s_ref[0, c] = scores(k0, c)

    def pair(jj, carry):
        fold(2 * jj, 0, boff_ref, k_ref[0, keys(2 * jj + 1), :])
        fold(2 * jj + 1, 1, boff_ref, k_ref[0, keys(2 * jj + 2), :])
        return carry

    lax.fori_loop(0, qi // 2, pair, 0)

    @pl.when(qi % 2 == 1)
    def _():
        fold(qi - 1, 0, boff_ref, k_ref[0, keys(qi), :])
        fold(qi, 1, bdiag_ref, None)

    @pl.when(qi % 2 == 0)
    def _():
        fold(qi, 0, bdiag_ref, None)

    lamv = lamv_ref[...]
    lam = (jnp.exp(jnp.sum(lamv[0:1] * lamv[1:2], axis=-1, keepdims=True))
           - jnp.exp(jnp.sum(lamv[2:3] * lamv[3:4], axis=-1, keepdims=True)) + lambda_init)
    heads = []
    for h in range(DIFF_HEADS):
        a0 = acc_ref[2 * h]
        a1 = acc_ref[2 * h + 1]
        heads.append(a0[0:HEAD_DIM] / a0[HEAD_DIM:HEAD_DIM + 1]
                     - lam * (a1[0:HEAD_DIM] / a1[HEAD_DIM:HEAD_DIM + 1]))
    o = jnp.concatenate(heads, axis=0).T
    ms = _dot_split(o * o, hsum_ref[...]) * (1.0 / HEAD_DIM)
    o = o * lax.rsqrt(ms + DIFF_SUBLN_EPS) * gain_ref[...]
    o_ref[0] = (o * (1.0 - lambda_init)).astype(o_ref.dtype)


def _diff_attention(p_qk, p_vt, lamv, subln_g, lambda_init):
    B, S, _ = p_qk.shape
    t = DIFF_TILE
    assert S % t == 0
    W = BRANCH_WIDTH
    boff, bdiag = (jnp.asarray(b) for b in _diff_bias())
    hsum = jnp.asarray(_head_sum_mat(W, HEAD_DIM), BF16)
    gain = jnp.tile(subln_g.reshape(1, HEAD_DIM), (1, DIFF_HEADS))
    single = lambda shape: pl.BlockSpec(shape, lambda b, i: (0,) * len(shape), pipeline_mode=pl.Buffered(1))
    return pl.pallas_call(
        functools.partial(_diff_kernel, lambda_init=lambda_init),
        grid=(B, S // t),
        in_specs=[pl.BlockSpec((1, t, W), lambda b, i: (b, i, 0)), pl.BlockSpec((1, S, W), lambda b, i: (b, 0, 1)),
                  pl.BlockSpec((W, S), lambda b, i: (0, b)),
                  single(boff.shape), single(bdiag.shape), _const_spec(lamv.shape), _const_spec((1, W)),
                  _const_spec((W, W))],
        out_specs=pl.BlockSpec((1, t, W), lambda b, i: (b, i, 0)),
        out_shape=jax.ShapeDtypeStruct((B, S, W), BF16),
        scratch_shapes=[pltpu.VMEM((2 * DIFF_HEADS, 8, t), F32), pltpu.VMEM((2 * DIFF_HEADS, DIFF_ACC_ROWS, t), F32),
                        pltpu.VMEM((2, 2 * DIFF_HEADS, t, t), F32), pltpu.VMEM((2 * DIFF_HEADS, LANES, t), BF16)],
        compiler_params=_params("parallel", "arbitrary"),
        name="diffattn",
    )(p_qk, p_qk, p_vt, boff, bdiag, lamv, gain, hsum)


def _conformer_tile(p, first, u_ref, w_ref, b_ref, g_ref, beta_ref, o_ref):
    ts = p.shape[0]
    Cc = CONV_CHANNELS
    halo = CONV_HALO
    rows = CONV_ROWS

    @pl.when(first)
    def _():
        u_ref[0:halo, :] = jnp.zeros((halo, Cc), F32)

    @pl.when(jnp.logical_not(first))
    def _():
        u_ref[0:halo, :] = u_ref[ts:ts + halo, :]

    u_ref[halo:halo + ts, :] = p[:, :Cc] * _sigmoid(p[:, Cc:])
    first_tap = halo - CONV_WIDTH + 1
    for r0 in range(0, ts, rows):
        halves = []
        for lanes in (slice(0, LANES), slice(LANES, Cc)):
            acc = jnp.zeros((rows, LANES), F32) + b_ref[:, lanes]
            win = u_ref[r0:r0 + rows + halo, lanes]
            for b in range(8):
                wb = win if b == 0 else pltpu.roll(win, rows + halo - b, 0)
                for a in range(halo // 8 + 1):
                    j = 8 * a + b - first_tap
                    if 0 <= j < CONV_WIDTH:
                        acc = acc + w_ref[j:j + 1, lanes] * wb[8 * a:8 * a + rows]
            halves.append(acc)
        acc = jnp.concatenate(halves, axis=1)
        mu = jnp.mean(acc, axis=-1, keepdims=True)
        d = acc - mu
        var = jnp.mean(d * d, axis=-1, keepdims=True)
        y = d * lax.rsqrt(var + CONV_LN_EPS) * g_ref[...] + beta_ref[...]
        o_ref[r0:r0 + rows, :] = (y * _sigmoid(y)).astype(o_ref.dtype)


def _merge_kernel(x_ref, ya_ref, yb_ref, yc_ref, yd_ref,
                  gpre_ref, gpost_ref, wg_ref, gb_ref, wb_ref, wo_ref, out_ref):
    tm = x_ref.shape[0]
    for r0 in range(0, tm, tm // ROW_GROUPS):
        rows = slice(r0, r0 + tm // ROW_GROUPS)
        x = x_ref[rows, :]
        h = _rms(x, gpre_ref[...]).astype(BF16)
        ys = (ya_ref[rows, :], yb_ref[rows, :], yc_ref[rows, :], yd_ref[rows, :])
        merged = None
        for n in range(N_BRANCHES):
            gate = _sigmoid(_dot(h, wg_ref[:, n * D_MODEL:(n + 1) * D_MODEL]) + gb_ref[n])
            term = gate * _dot(ys[n], wb_ref[n])
            merged = term if merged is None else merged + term
        y = _dot(merged.astype(BF16), wo_ref[...])
        out_ref[rows, :] = x + _rms(y, gpost_ref[...])


def _merge(x2, ya, yb, yc, yd, gpre, gpost, w_gate, gate_bias, w_branch, w_out):
    T = x2.shape[0]
    tm = TOKEN_TILE
    D = D_MODEL
    W = BRANCH_WIDTH
    row = lambda w: pl.BlockSpec((tm, w), lambda i: (i, 0))
    single = lambda shape: pl.BlockSpec(shape, lambda i: (0,) * len(shape), pipeline_mode=pl.Buffered(1))
    return pl.pallas_call(
        _merge_kernel,
        grid=(T // tm,),
        in_specs=[row(D)] + [row(W)] * 4 + [_const_spec((1, D)), _const_spec((1, D)),
                  single((D, N_BRANCHES * D)), _const_spec((N_BRANCHES, 1, D)), single((N_BRANCHES, W, D)),
                  single((D, D))],
        out_specs=row(D),
        out_shape=jax.ShapeDtypeStruct((T, D), F32),
        compiler_params=_params("parallel"),
        name="merge",
    )(x2, ya, yb, yc, yd, gpre, gpost, w_gate, gate_bias, w_branch, w_out)


def _ffn_kernel(x_ref, gpre_ref, gpost_ref, wg_ref, wu_ref, wd_ref, out_ref):
    tm = x_ref.shape[0]
    for r0 in range(0, tm, tm // ROW_GROUPS):
        rows = slice(r0, r0 + tm // ROW_GROUPS)
        x = x_ref[rows, :]
        h = _rms(x, gpre_ref[...]).astype(BF16)
        g = _dot(h, wg_ref[...])
        u = _dot(h, wu_ref[...])
        a = (g * _sigmoid(g) * u).astype(BF16)
        f = _dot(a, wd_ref[...])
        out_ref[rows, :] = x + _rms(f, gpost_ref[...])


def _ffn(x2, gpre, gpost, w_gate, w_up, w_down):
    T = x2.shape[0]
    tm = TOKEN_TILE
    D = D_MODEL
    row = pl.BlockSpec((tm, D), lambda i: (i, 0))
    single = lambda shape: pl.BlockSpec(shape, lambda i: (0,) * len(shape), pipeline_mode=pl.Buffered(1))
    return pl.pallas_call(
        _ffn_kernel,
        grid=(T // tm,),
        in_specs=[row, _const_spec((1, D)), _const_spec((1, D)), single((D, D_FF)), single((D, D_FF)),
                  single((D_FF, D))],
        out_specs=row,
        out_shape=jax.ShapeDtypeStruct((T, D), F32),
        compiler_params=_params("parallel"),
        name="swiglu",
    )(x2, gpre, gpost, w_gate, w_up, w_down)


def _layer(x2, B, S, l, prm):
    D = D_MODEL
    W = BRANCH_WIDTH
    row = lambda a: a.reshape(1, -1)
    w_in = prm["w_in"][l]
    w_dil = w_in[:, RWKV_COLS:RWKV_COLS + DIL_COLS].reshape(D, 3, DIL_GROUPS, W)
    w_dil = jnp.concatenate([w_dil[:, 1], w_dil[:, 2], w_dil[:, 0]], axis=-1).transpose(1, 0, 2)
    diff0 = RWKV_COLS + DIL_COLS
    w_mix = jnp.concatenate([w_in[:, :RWKV_COLS]] + [w_dil[g] for g in range(DIL_GROUPS)]
                            + [w_in[:, diff0:diff0 + DIFF_QK_COLS], w_in[:, diff0 + DIFF_COLS:MIX_COLS]],
                            axis=1).astype(BF16)
    w_vt = w_in[:, diff0 + DIFF_QK_COLS:diff0 + DIFF_COLS].T.astype(BF16)
    w_gate = w_in[:, MIX_COLS:].astype(BF16)

    p_rwkv, pd0, pd1, pd2, p_qk, yd, p_vt = _inproj(
        x2, S, row(prm["norm_mix_pre"][l]), w_mix, w_vt, prm["conv_dw_w"][l].reshape(CONV_WIDTH, CONV_CHANNELS),
        row(prm["conv_dw_b"][l]), row(prm["conv_ln_g"][l]), row(prm["conv_ln_b"][l]))

    zeros = jnp.zeros((64, W), F32)
    wa_up = jnp.concatenate([jnp.concatenate([prm["rwkv_w_up"][l], zeros], axis=1),
                             jnp.concatenate([zeros, prm["rwkv_a_up"][l]], axis=1)], axis=0).astype(BF16)
    ya = _rwkv(p_rwkv.reshape(B, S, RWKV_COLS), row(prm["rwkv_mu"][l]), row(prm["rwkv_w0"][l]),
               row(prm["rwkv_a0"][l]), wa_up, prm["rwkv_g_up"][l].astype(BF16), row(prm["rwkv_k_k"][l]),
               row(prm["rwkv_k_a"][l]), row(prm["rwkv_r_k"][l]), row(prm["rwkv_ln_g"][l]),
               row(prm["rwkv_ln_b"][l]))

    group_w = DIL_COLS // DIL_GROUPS
    yb = _dilated(pd0.reshape(B, S, group_w), pd1.reshape(B, S, group_w), pd2.reshape(B, S, group_w))

    lambda_init = 0.8 - 0.6 * math.exp(-0.3 * l)
    lamv = jnp.stack([prm["diff_lam_q1"][l], prm["diff_lam_k1"][l], prm["diff_lam_q2"][l], prm["diff_lam_k2"][l]])
    yc = _diff_attention(p_qk.reshape(B, S, DIFF_QK_COLS), p_vt, lamv, prm["diff_subln_g"][l], lambda_init)

    x2 = _merge(x2, ya.reshape(B * S, W), yb.reshape(B * S, W), yc.reshape(B * S, W), yd.reshape(B * S, W),
                row(prm["norm_mix_pre"][l]), row(prm["norm_mix_post"][l]), w_gate,
                prm["gate_bias"][l].reshape(N_BRANCHES, 1, D), prm["w_branch"][l].astype(BF16),
                prm["w_out"][l].astype(BF16))

    return _ffn(x2, row(prm["norm_ffn_pre"][l]), row(prm["norm_ffn_post"][l]), prm["ffn_w_gate"][l].astype(BF16),
                prm["ffn_w_up"][l].astype(BF16), prm["ffn_w_down"][l].astype(BF16))


def kernel(x, norm_mix_pre, norm_mix_post, norm_ffn_pre, norm_ffn_post, w_in, gate_bias, rwkv_mu, rwkv_w0, rwkv_w_up, rwkv_a0, rwkv_a_up, rwkv_g_up, rwkv_k_k, rwkv_k_a, rwkv_r_k, rwkv_ln_g, rwkv_ln_b, diff_lam_q1, diff_lam_k1, diff_lam_q2, diff_lam_k2, diff_subln_g, conv_dw_w, conv_dw_b, conv_ln_g, conv_ln_b, w_branch, w_out, ffn_w_gate, ffn_w_up, ffn_w_down):
    prm = dict(locals())
    B, S, D = x.shape
    x2 = x.reshape(B * S, D)
    for l in range(w_in.shape[0]):
        x2 = _layer(x2, B, S, l, prm)
    return x2.reshape(B, S, D)
```

```python
import functools
import math

import numpy as np
import jax
import jax.numpy as jnp
from jax import lax
from jax.experimental import pallas as pl
from jax.experimental.pallas import tpu as pltpu

F32 = jnp.float32
BF16 = jnp.bfloat16
LANES = 128

D_MODEL = 1024
HEAD_DIM = 64
BRANCH_WIDTH = 256
N_BRANCHES = 4

RWKV_HEADS = 4
RWKV_WIDTH = 256
RWKV_COLS = 1024
RWKV_DECAY_SCALE = 0.606531
RWKV_LN_EPS = 64e-5
RWKV_DECAY_RANK = 64
RWKV_ICL_RANK = 64
RWKV_GATE_RANK = 128
RWKV_LORA_COLS = RWKV_DECAY_RANK + RWKV_ICL_RANK
RWKV_CHUNK = 64
RWKV_TILE = 256
RWKV_SEQS = 2

DIL_PATTERNS = ((128, 1), (512, 4), (2048, 16))
DIL_GROUPS = 3
DIL_HEADS = 12
DIL_COLS = 2304
DIL_BLOCK = 128
DIL_TILE = 2048
DIL_SUB = 512
DIL_G0_BLOCKS_PER_TRIP = 5
DIL_G2_BLOCKS_PER_TRIP = 8

DIFF_HEADS = 4
DIFF_QK_DIM = 32
DIFF_COLS = 768
DIFF_TILE = 512
DIFF_QK_COLS = 512
DIFF_ACC_ROWS = HEAD_DIM + 16
DIFF_SUBLN_EPS = 1e-5

CONV_CHANNELS = 256
CONV_WIDTH = 31
CONV_COLS = 512
CONV_LN_EPS = 1e-5
CONV_HALO = 32
CONV_ROWS = 64

MIX_COLS = RWKV_COLS + DIL_COLS + DIFF_COLS + CONV_COLS
D_FF = 2816
NORM_EPS = 1e-6
NEG_INF = -1e30
_LOG2E = 1.4426950408889634

VMEM_LIMIT_BYTES = 56 * 1024 * 1024
TOKEN_TILE = 512
ROW_GROUPS = 2


def _dot(a, b):
    return jnp.dot(a, b, preferred_element_type=F32)


def _dot_nt(a, b):
    return lax.dot_general(a, b, (((1,), (1,)), ((), ())), preferred_element_type=F32)


def _dot_tn(a, b):
    return lax.dot_general(a, b, (((0,), (0,)), ((), ())), preferred_element_type=F32)


def _dot_split(x, m):
    hi = x.astype(BF16)
    lo = (x - hi.astype(F32)).astype(BF16)
    return _dot(hi, m) + _dot(lo, m)


def _dot_split_rhs(m, x):
    hi = x.astype(BF16)
    lo = (x - hi.astype(F32)).astype(BF16)
    return _dot(m, hi) + _dot(m, lo)


def _rms(x, gain):
    return x * lax.rsqrt(jnp.mean(x * x, axis=-1, keepdims=True) + NORM_EPS) * gain


def _sigmoid(x):
    return 1.0 / (1.0 + jnp.exp(-x))


def _const_spec(shape):
    nd = len(shape)
    return pl.BlockSpec(shape, lambda *_: (0,) * nd)


def _params(*sem):
    return pltpu.CompilerParams(dimension_semantics=sem, vmem_limit_bytes=VMEM_LIMIT_BYTES)


def _inproj_kernel(x_ref, g_ref, w_ref, wvt_ref, cw_ref, cb_ref, cg_ref, cbeta_ref,
                   o_rwkv, o_dil0, o_dil1, o_dil2, o_diff, o_conv, o_vt, h_ref, u_ref, *, tiles_per_seq):
    tm = x_ref.shape[0]
    hf = _rms(x_ref[...], g_ref[...])
    h = hf.astype(BF16)
    conv0 = w_ref.shape[1] - CONV_COLS
    _conformer_tile(_dot(h, w_ref[:, conv0:]), pl.program_id(0) % tiles_per_seq == 0, u_ref, cw_ref, cb_ref, cg_ref,
                    cbeta_ref, o_conv)
    slabs = D_MODEL // LANES
    for j in range(slabs):
        h_ref[j] = hf[:, j * LANES:(j + 1) * LANES]

    def permuted(d):
        cols = [jnp.concatenate([h_ref[j, pl.ds(r, tm // d, stride=d), :] for r in range(d)], axis=0)
                for j in range(slabs)]
        return jnp.concatenate(cols, axis=1).astype(BF16)

    group_w = DIL_COLS // DIL_GROUPS
    col = 0
    for o_ref, width, lhs in ((o_rwkv, RWKV_COLS, h), (o_dil0, group_w, h),
                              (o_dil1, group_w, permuted(DIL_PATTERNS[1][1])),
                              (o_dil2, group_w, permuted(DIL_PATTERNS[2][1])),
                              (o_diff, DIFF_QK_COLS, h)):
        o_ref[...] = _dot(lhs, w_ref[:, col:col + width]).astype(o_ref.dtype)
        col += width
    o_vt[...] = _dot_nt(wvt_ref[...], h).astype(o_vt.dtype)


def _inproj(x2, seq_len, gain, w_mix, w_vt, dw_w, dw_b, ln_g, ln_b):
    T = x2.shape[0]
    tm = TOKEN_TILE
    assert tm == DIL_SUB and seq_len % tm == 0
    group_w = DIL_COLS // DIL_GROUPS
    W = BRANCH_WIDTH
    Cc = CONV_CHANNELS
    row = lambda w: pl.BlockSpec((tm, w), lambda i: (i, 0))
    bf = lambda w: jax.ShapeDtypeStruct((T, w), BF16)
    vec = _const_spec((1, Cc))
    return pl.pallas_call(
        functools.partial(_inproj_kernel, tiles_per_seq=seq_len // tm),
        grid=(T // tm,),
        in_specs=[row(D_MODEL), _const_spec((1, D_MODEL)), _const_spec((D_MODEL, MIX_COLS - W)),
                  _const_spec((W, D_MODEL)), _const_spec((CONV_WIDTH, Cc)), vec, vec, vec],
        out_specs=[row(RWKV_COLS), row(group_w), row(group_w), row(group_w), row(DIFF_QK_COLS), row(Cc),
                   pl.BlockSpec((W, tm), lambda i: (0, i))],
        out_shape=[jax.ShapeDtypeStruct((T, RWKV_COLS), F32), bf(group_w), bf(group_w), bf(group_w),
                   bf(DIFF_QK_COLS), bf(Cc), jax.ShapeDtypeStruct((W, T), BF16)],
        scratch_shapes=[pltpu.VMEM((D_MODEL // LANES, tm, LANES), F32), pltpu.VMEM((tm + CONV_HALO, Cc), F32)],
        compiler_params=_params("arbitrary"),
        name="inproj",
    )(x2, gain, w_mix, w_vt, dw_w, dw_b, ln_g, ln_b)


_MASK_STRICT, _MASK_INCL, _MASK_EYE, _MASK_LEVEL0 = 0, 1, 2, 3
_INV_LEVELS = (1, 2, 4, 8, 16, 32)


def _rwkv_masks():
    n = RWKV_HEADS * RWKV_CHUNK
    r = np.arange(n)[:, None]
    c = np.arange(n)[None, :]
    same = (r // RWKV_CHUNK) == (c // RWKV_CHUNK)
    out = [same & (r > c), same & (r >= c), r == c]
    for s in _INV_LEVELS:
        out.append((r // (2 * s) == c // (2 * s)) & (r % (2 * s) >= s) & (c % (2 * s) < s))
    head = (r // RWKV_CHUNK) == (c // HEAD_DIM)
    out.append(head)
    return np.stack(out).astype(np.float32)


_MASK_HEAD = 3 + len(_INV_LEVELS)


def _chunk_sum_mats(ts):
    r = np.arange(ts)[:, None]
    c = np.arange(ts)[None, :]
    same = (r // RWKV_CHUNK) == (c // RWKV_CHUNK)
    return np.stack([same & (c <= r), same]).astype(np.float32)


def _head_sum_mat(width, group):
    r = np.arange(width)[:, None]
    c = np.arange(width)[None, :]
    return ((r // group) == (c // group)).astype(np.float32)


def _rwkv_kernel(p_ref, mu_ref, w0_ref, a0_ref, wa_ref, gup_ref, kk_ref, ka_ref, rk_ref, lng_ref, lnb_ref,
                 masks_ref, csum_ref, hsum_ref, o_ref,
                 state_ref, prev_ref, at_ref, bt_ref, kt_ref, rt_ref, bh_ref, kh_ref, v_ref, ec_ref, oc_ref,
                 lhs_ref, upd_ref, vr_ref, arb_ref, ut_ref, arkv_ref):
    nseq, ts = p_ref.shape[0], p_ref.shape[1]
    W = RWKV_WIDTH
    C = RWKV_CHUNK
    n = RWKV_HEADS * C
    nc = ts // C

    @pl.when(pl.program_id(1) == 0)
    def _():
        state_ref[...] = jnp.zeros_like(state_ref)
        prev_ref[...] = jnp.zeros_like(prev_ref)

    hsum = hsum_ref[...]
    kept = []
    for q in range(nseq):
        p = p_ref[q]
        row = lax.broadcasted_iota(jnp.int32, p.shape, 0)
        p_prev = jnp.where(row == 0, prev_ref[q], pltpu.roll(p, 1, 0))
        prev_ref[q] = p[ts - 1:ts, :]
        p = p + (p_prev - p) * mu_ref[...]

        r = p[:, 0:W]
        k = p[:, W:2 * W]
        v = p[:, 2 * W:3 * W]
        zwa = p[:, 3 * W:3 * W + RWKV_LORA_COLS]
        zg = p[:, 3 * W + RWKV_LORA_COLS:]
        lane = lax.broadcasted_iota(jnp.int32, zwa.shape, 1)
        tz = jnp.where(lane < RWKV_DECAY_RANK, jnp.tanh(zwa), zwa)
        wa = _dot(tz.astype(BF16), wa_ref[...])
        logw = -RWKV_DECAY_SCALE * _sigmoid(w0_ref[...] + wa[:, :W])
        alpha = _sigmoid(a0_ref[...] + wa[:, W:])
        g = _dot(_sigmoid(zg).astype(BF16), gup_ref[...])

        kk = k * kk_ref[...]
        kk = kk * lax.rsqrt(jnp.maximum(_dot_split(kk * kk, hsum), 1e-24))
        k2 = k * (1.0 + (alpha - 1.0) * ka_ref[...])
        bvec = kk * alpha

        cum = _dot_split_rhs(csum_ref[0], logw)
        cum_end = _dot_split_rhs(csum_ref[1], logw)
        e_neg = jnp.exp(-cum)
        e_rem = jnp.exp(cum_end - cum)
        at_ref[q] = -kk * jnp.exp(cum - logw)
        bt_ref[q] = bvec * e_neg
        kt_ref[q] = k2 * e_neg
        rt_ref[q] = r * jnp.exp(cum)
        bh_ref[q] = bvec * e_rem
        kh_ref[q] = k2 * e_rem
        v_ref[q] = v
        ec_ref[q] = jnp.exp(cum_end)
        kept.append((r * k2 * rk_ref[...], v, g))

    def tile4(x):
        x4 = jnp.concatenate([x, x, x, x], axis=0)
        return jnp.where(masks_ref[_MASK_HEAD] > 0, x4, 0.0).astype(BF16)

    def keep(m, x):
        return jnp.where(masks_ref[m] > 0, x, 0.0)

    chains = [(q, c) for q in range(nseq) for c in range(nc)]
    rows = [slice(c * C, (c + 1) * C) for _, c in chains]
    at = [tile4(at_ref[q, r, :]) for (q, _), r in zip(chains, rows)]
    vr = [tile4(v_ref[q, r, :]) for (q, _), r in zip(chains, rows)]
    a_ab, akv = [], []
    for j, ((q, c), r) in enumerate(zip(chains, rows)):
        rt = tile4(rt_ref[q, r, :])
        lhs = jnp.concatenate([at[j], rt], axis=0)
        rhs = jnp.concatenate([tile4(bt_ref[q, r, :]), tile4(kt_ref[q, r, :])], axis=0)
        a = _dot_nt(lhs, rhs)
        a_ab.append(keep(_MASK_STRICT, a[:n, :n]))
        arb_ref[q, c] = keep(_MASK_INCL, a[n:, :n]).astype(BF16)
        akv_arkv = _dot(jnp.concatenate([keep(_MASK_STRICT, a[:n, n:]), keep(_MASK_INCL, a[n:, n:])],
                                        axis=0).astype(BF16), vr[j])
        akv.append(akv_arkv[:n].astype(BF16))
        arkv_ref[q, c] = akv_arkv[n:]
        lhs_ref[q, c, n:2 * n, :] = rt
        vr_ref[q, c] = vr[j]
        upd_ref[q, c, 0:n, :] = tile4(bh_ref[q, r, :])
        upd_ref[q, c, n:2 * n, :] = tile4(kh_ref[q, r, :])

    def lower_rows(x, s):
        return jnp.concatenate([x[k * 2 * s + s:(k + 1) * 2 * s] for k in range(n // (2 * s))], axis=0)

    def scatter_lower(y, s):
        zero = jnp.zeros((s, y.shape[1]), y.dtype)
        return jnp.concatenate([p for k in range(n // (2 * s)) for p in (zero, y[k * s:(k + 1) * s])], axis=0)

    t = [masks_ref[_MASK_EYE] + keep(_MASK_LEVEL0, a) for a in a_ab]
    for li in range(1, len(_INV_LEVELS)):
        s = _INV_LEVELS[li]
        tb = [x.astype(BF16) for x in t]
        x_lvl = [keep(_MASK_LEVEL0 + li, a) for a in a_ab]
        if s % 8 == 0:
            xt = [scatter_lower(_dot(lower_rows(x, s).astype(BF16), y), s).astype(BF16) for x, y in zip(x_lvl, tb)]
            t = [x + scatter_lower(_dot(lower_rows(x, s).astype(BF16), z), s) for x, z in zip(t, xt)]
        else:
            xt = [_dot(x.astype(BF16), y).astype(BF16) for x, y in zip(x_lvl, tb)]
            t = [x + _dot(y, z) for x, y, z in zip(t, tb, xt)]
    for j, (q, c) in enumerate(chains):
        tb = t[j].astype(BF16)
        lhs_ref[q, c, 0:n, :] = _dot(tb, at[j]).astype(BF16)
        ut_ref[q, c] = _dot(tb, akv[j])

    for c in range(nc):
        s = [state_ref[q] for q in range(nseq)]
        g_us = [_dot_nt(lhs_ref[q, c], s[q].astype(BF16)) for q in range(nseq)]
        ub = [(g_us[q][:n] + ut_ref[q, c]).astype(BF16) for q in range(nseq)]
        o = [g_us[q][n:] + _dot(arb_ref[q, c], ub[q]) + arkv_ref[q, c] for q in range(nseq)]
        upd = [_dot_tn(jnp.concatenate([ub[q], vr_ref[q, c]], axis=0), upd_ref[q, c]) for q in range(nseq)]
        for q in range(nseq):
            oc_ref[q, c * C:(c + 1) * C, :] = o[q][0:C] + o[q][C:2 * C] + o[q][2 * C:3 * C] + o[q][3 * C:4 * C]
            state_ref[q] = s[q] * ec_ref[q, c * C:c * C + 1, :] + upd[q]

    hmean = hsum * (1.0 / HEAD_DIM)
    for q, (rk2, v, g) in enumerate(kept):
        o = oc_ref[q]
        mean = _dot_split(o, hmean)
        d = o - mean
        var = _dot_split(d * d, hmean)
        o = d * lax.rsqrt(var + RWKV_LN_EPS) * lng_ref[...] + lnb_ref[...]
        bonus = _dot_split(rk2, hsum)
        o_ref[q] = ((o + bonus * v) * g).astype(o_ref.dtype)


def _rwkv(p, mu, w0, a0, wa_up, g_up, k_k, k_a, r_k, ln_g, ln_b):
    B, S, _ = p.shape
    ts = RWKV_TILE
    nseq = RWKV_SEQS
    assert B % nseq == 0 and S % ts == 0
    W = RWKV_WIDTH
    n = RWKV_HEADS * RWKV_CHUNK
    nc = ts // RWKV_CHUNK
    masks = jnp.asarray(_rwkv_masks())
    csum = jnp.asarray(_chunk_sum_mats(ts), BF16)
    hsum = jnp.asarray(_head_sum_mat(W, HEAD_DIM), BF16)
    vec = _const_spec((1, W))
    buf = lambda: pltpu.VMEM((nseq, ts, W), F32)
    return pl.pallas_call(
        _rwkv_kernel,
        grid=(B // nseq, S // ts),
        in_specs=[pl.BlockSpec((nseq, ts, RWKV_COLS), lambda b, s: (b, s, 0)),
                  _const_spec((1, RWKV_COLS)), vec, vec, _const_spec((RWKV_LORA_COLS, 2 * W)),
                  _const_spec((RWKV_GATE_RANK, W)),
                  vec, vec, vec, vec, vec,
                  _const_spec(masks.shape), _const_spec(csum.shape), _const_spec(hsum.shape)],
        out_specs=pl.BlockSpec((nseq, ts, W), lambda b, s: (b, s, 0)),
        out_shape=jax.ShapeDtypeStruct((B, S, W), BF16),
        scratch_shapes=[pltpu.VMEM((nseq, n, n), F32), pltpu.VMEM((nseq, 1, RWKV_COLS), F32)]
                       + [buf() for _ in range(9)]
                       + [pltpu.VMEM((nseq, nc, 2 * n, n), BF16), pltpu.VMEM((nseq, nc, 2 * n, n), BF16),
                          pltpu.VMEM((nseq, nc, n, n), BF16), pltpu.VMEM((nseq, nc, n, n), BF16),
                          pltpu.VMEM((nseq, nc, n, n), F32), pltpu.VMEM((nseq, nc, n, n), F32)],
        compiler_params=_params("parallel", "arbitrary"),
        name="rwkv7",
    )(p, mu, w0, a0, wa_up, g_up, k_k, k_a, r_k, ln_g, ln_b, masks, csum, hsum)


def _alibi_slopes(n):
    return 2.0 ** (-8.0 * np.arange(1, n + 1) / n)


def _dil_bias(group):
    window, dilation = DIL_PATTERNS[group]
    n = window // dilation
    slopes = _alibi_slopes(DIL_HEADS).reshape(DIL_GROUPS, -1)[group]
    rel = n + np.arange(n)[:, None] - np.arange(2 * n)[None, :]
    valid = (rel >= 0) & (rel <= n)
    bias = -slopes[:, None, None] * (dilation * rel).astype(np.float64)[None]
    return np.where(valid[None], bias, NEG_INF).astype(np.float32)


def _dil_attend(blocks, bias_of, front):
    shape = blocks[0][0].shape
    heads = shape[1] // HEAD_DIM
    head = lax.broadcasted_iota(jnp.int32, shape, 1) // HEAD_DIM
    pairs = [(b, h) for b in range(len(blocks)) for h in range(heads)]
    s = [_dot_nt(jnp.where(head == h, blocks[b][0], jnp.zeros(shape, BF16)), blocks[b][1]) + bias_of(h)
         for b, h in pairs]
    if front is not None:
        s = [jnp.where(front, NEG_INF, x) for x in s]
    m = [jnp.max(x, axis=-1, keepdims=True) for x in s]
    e = [jnp.exp2(x - mx) for x, mx in zip(s, m)]
    l = [jnp.sum(x, axis=-1, keepdims=True) for x in e]
    per_half = LANES // HEAD_DIM
    pv = [_dot(x.astype(BF16), blocks[b][2][:, (h // per_half) * LANES:(h // per_half + 1) * LANES])
          for x, (b, h) in zip(e, pairs)]
    low = lax.broadcasted_iota(jnp.int32, (shape[0], LANES), 1) < HEAD_DIM
    results = []
    for b in range(len(blocks)):
        outs, lses = [], []
        for half in range(heads // per_half):
            i0, i1 = b * heads + half * per_half, b * heads + half * per_half + 1
            outs.append(jnp.where(low, pv[i0] / l[i0], pv[i1] / l[i1]))
            lses.append(jnp.where(low, m[i0] + jnp.log(l[i0]) * _LOG2E, m[i1] + jnp.log(l[i1]) * _LOG2E))
        results.append((outs, lses))
    return results


def _dil_kernel(c0_ref, c1_ref, c2_ref, p0_ref, p1_ref, p2_ref, b0_ref, b1_ref, b2_ref, y_ref, og_ref, lg_ref):
    n = DIL_BLOCK
    W = BRANCH_WIDTH
    sub = DIL_SUB
    first = pl.program_id(1) == 0
    front = jnp.logical_and(first, lax.broadcasted_iota(jnp.int32, (n, 2 * n), 1) < n)
    qscale = HEAD_DIM ** -0.5 * _LOG2E
    kcols, vcols, qcols = slice(0, W), slice(W, 2 * W), slice(2 * W, 3 * W)

    def scaled(q):
        return (q.astype(F32) * qscale).astype(BF16)

    def put(g, rows, out, lse):
        for half in range(W // LANES):
            og_ref[g, half, rows, :] = out[half]
            lg_ref[g, half, rows, :] = lse[half]

    def block_of(prev, cur):
        return (scaled(cur[:, qcols]), jnp.concatenate([prev[:, kcols], cur[:, kcols]], axis=0),
                jnp.concatenate([prev[:, vcols], cur[:, vcols]], axis=0))

    (out, lse), = _dil_attend([block_of(p0_ref[0], c0_ref[0, 0:n, :])], lambda h: b0_ref[h], front)
    put(0, pl.ds(0, n), out, lse)
    per0 = DIL_G0_BLOCKS_PER_TRIP

    def g0_body(i, carry):
        blocks, rows = [], []
        for j in range(per0):
            r0 = pl.multiple_of((i * per0 + j) * n, n)
            kv = c0_ref[0, pl.ds(r0, 2 * n), 0:2 * W]
            blocks.append((scaled(c0_ref[0, pl.ds(r0 + n, n), qcols]), kv[:, kcols], kv[:, vcols]))
            rows.append(pl.ds(r0 + n, n))
        for r, (out, lse) in zip(rows, _dil_attend(blocks, lambda h: b0_ref[h], None)):
            put(0, r, out, lse)
        return carry

    assert (DIL_TILE // n - 1) % per0 == 0
    lax.fori_loop(0, (DIL_TILE // n - 1) // per0, g0_body, 0)

    d1 = DIL_PATTERNS[1][1]

    def g1_rows(u, r):
        return pl.ds(pl.multiple_of(u * sub + r * n, n), n)

    def g1_run(u, prevs, front_mask):
        blocks = [block_of(prevs[r], c1_ref[0, g1_rows(u, r), :]) for r in range(d1)]
        for r, (out, lse) in enumerate(_dil_attend(blocks, lambda h: b1_ref[h], front_mask)):
            put(1, pl.ds(u * sub + r, n, stride=d1), out, lse)

    g1_run(0, [p1_ref[0, r * n:(r + 1) * n, :] for r in range(d1)], front)

    def g1_body(u, carry):
        g1_run(u, [c1_ref[0, g1_rows(u - 1, r), 0:2 * W] for r in range(d1)], None)
        return carry

    lax.fori_loop(1, DIL_TILE // sub, g1_body, 0)

    d2 = DIL_PATTERNS[2][1]
    piece = sub // d2
    per2 = DIL_G2_BLOCKS_PER_TRIP

    def g2_body(i, carry):
        def rows_of(ref, r, cols):
            return jnp.concatenate(
                [ref[0, pl.ds(pl.multiple_of(u * sub + r * piece, piece), piece), cols]
                 for u in range(DIL_TILE // sub)], axis=0)
        rs = [i * per2 + j for j in range(per2)]
        blocks = [block_of(rows_of(p2_ref, r, slice(0, 2 * W)), rows_of(c2_ref, r, slice(0, 3 * W))) for r in rs]
        for r, (out, lse) in zip(rs, _dil_attend(blocks, lambda h: b2_ref[h], front)):
            put(2, pl.ds(r, n, stride=d2), out, lse)
        return carry

    assert d2 % per2 == 0
    lax.fori_loop(0, d2 // per2, g2_body, 0)

    rows_per = 256

    def combine(j, carry):
        rows = pl.ds(pl.multiple_of(j * rows_per, rows_per), rows_per)
        for half in range(W // LANES):
            ls = [lg_ref[g, half, rows, :] for g in range(DIL_GROUPS)]
            m = jnp.maximum(jnp.maximum(ls[0], ls[1]), ls[2])
            es = [jnp.exp2(x - m) for x in ls]
            num = es[0] * og_ref[0, half, rows, :] + es[1] * og_ref[1, half, rows, :] + es[2] * og_ref[2, half, rows, :]
            y_ref[0, rows, half * LANES:(half + 1) * LANES] = (num / (es[0] + es[1] + es[2])).astype(y_ref.dtype)
        return carry

    lax.fori_loop(0, DIL_TILE // rows_per, combine, 0)


def _dilated(pd0, pd1, pd2):
    B, S, _ = pd0.shape
    W = BRANCH_WIDTH
    T = DIL_TILE
    assert S % T == 0
    for (window, d) in DIL_PATTERNS:
        assert window // d == DIL_BLOCK
    assert DIL_PATTERNS[0][1] == 1 and DIL_PATTERNS[1][1] * DIL_BLOCK == DIL_SUB and DIL_PATTERNS[2][1] * DIL_BLOCK == T
    biases = [jnp.asarray(_dil_bias(g) * _LOG2E) for g in range(DIL_GROUPS)]
    cur = pl.BlockSpec((1, T, 3 * W), lambda b, i: (b, i, 0))
    prev = lambda rows: pl.BlockSpec((1, rows, 2 * W), lambda b, i: (b, jnp.maximum(i * (T // rows) - 1, 0), 0))
    return pl.pallas_call(
        _dil_kernel,
        grid=(B, S // T),
        in_specs=[cur, cur, cur, prev(DIL_BLOCK), prev(DIL_SUB), prev(T)] + [_const_spec(b.shape) for b in biases],
        out_specs=pl.BlockSpec((1, T, W), lambda b, i: (b, i, 0)),
        out_shape=jax.ShapeDtypeStruct((B, S, W), BF16),
        scratch_shapes=[pltpu.VMEM((DIL_GROUPS, W // LANES, T, LANES), F32)] * 2,
        compiler_params=_params("parallel", "arbitrary"),
        name="dilated",
    )(pd0, pd1, pd2, pd0, pd1, pd2, *biases)


def _diff_bias():
    t = DIFF_TILE
    slopes = _alibi_slopes(DIFF_HEADS) * _LOG2E
    rel = (np.arange(t)[None, :] - np.arange(t)[:, None]).astype(np.float64)
    off = -slopes[:, None, None] * rel[None]
    diag = np.where(rel[None] >= 0, off, NEG_INF)
    return off.astype(np.float32), diag.astype(np.float32)


def _diff_kernel(q_ref, k_ref, vt_ref, boff_ref, bdiag_ref, lamv_ref, gain_ref, hsum_ref, o_ref, m_ref, acc_ref,
                 s_ref, qt_ref, *, lambda_init):
    t = DIFF_TILE
    n_maps = 2 * DIFF_HEADS
    ones_rows = DIFF_ACC_ROWS - HEAD_DIM
    qi = pl.program_id(1)
    q = q_ref[0].astype(F32) * (DIFF_QK_DIM ** -0.5 * _LOG2E)
    maps_per_half = LANES // DIFF_QK_DIM
    lane = lax.broadcasted_iota(jnp.int32, (t, LANES), 1)
    for c in range(n_maps):
        half = c // maps_per_half
        qh = jnp.where(lane // DIFF_QK_DIM == c % maps_per_half, q[:, half * LANES:(half + 1) * LANES], 0.0)
        qt_ref[c] = qh.T.astype(BF16)
    slopes = _alibi_slopes(DIFF_HEADS) * _LOG2E

    def scores(k, c):
        half = c // maps_per_half
        return _dot(k[:, half * LANES:(half + 1) * LANES], qt_ref[c])

    m_ref[...] = jnp.full(m_ref.shape, NEG_INF, F32)
    acc_ref[...] = jnp.zeros_like(acc_ref)

    def keys(kb):
        return pl.ds(pl.multiple_of(kb * t, t), t)

    def fold(kb, slot, bias_ref, k_next):
        ones = jnp.ones((ones_rows, t), BF16)
        vt = [jnp.concatenate([vt_ref[h * HEAD_DIM:(h + 1) * HEAD_DIM, keys(kb)], ones], axis=0)
              for h in range(DIFF_HEADS)]
        shift = ((qi - kb) * t).astype(F32)
        for c in range(n_maps):
            if k_next is not None:
                s_ref[1 - slot, c] = scores(k_next, c)
            block_bias = float(slopes[c // 2]) * shift
            s = s_ref[slot, c] + bias_ref[c // 2]
            m_prev = m_ref[c, 0:1, :]
            m_new = jnp.maximum(m_prev, jnp.max(s, axis=0, keepdims=True) - block_bias)
            e = jnp.exp2(s - (m_new + block_bias)).astype(BF16)
            m_ref[c] = jnp.broadcast_to(m_new, m_ref.shape[1:])
            acc_ref[c] = acc_ref[c] * jnp.exp2(m_prev - m_new) + _dot(vt[c // 2], e)

    k0 = k_ref[0, keys(0), :]
    for c in range(n_maps):
        s_ref[0, c] = scores(k0, c)

    def pair(jj, carry):
        fold(2 * jj, 0, boff_ref, k_ref[0, keys(2 * jj + 1), :])
        fold(2 * jj + 1, 1, boff_ref, k_ref[0, keys(2 * jj + 2), :])
        return carry

    lax.fori_loop(0, qi // 2, pair, 0)

    @pl.when(qi % 2 == 1)
    def _():
        fold(qi - 1, 0, boff_ref, k_ref[0, keys(qi), :])
        fold(qi, 1, bdiag_ref, None)

    @pl.when(qi % 2 == 0)
    def _():
        fold(qi, 0, bdiag_ref, None)

    lamv = lamv_ref[...]
    lam = (jnp.exp(jnp.sum(lamv[0:1] * lamv[1:2], axis=-1, keepdims=True))
           - jnp.exp(jnp.sum(lamv[2:3] * lamv[3:4], axis=-1, keepdims=True)) + lambda_init)
    heads = []
    for h in range(DIFF_HEADS):
        a0 = acc_ref[2 * h]
        a1 = acc_ref[2 * h + 1]
        heads.append(a0[0:HEAD_DIM] / a0[HEAD_DIM:HEAD_DIM + 1]
                     - lam * (a1[0:HEAD_DIM] / a1[HEAD_DIM:HEAD_DIM + 1]))
    o = jnp.concatenate(heads, axis=0).T
    ms = _dot_split(o * o, hsum_ref[...]) * (1.0 / HEAD_DIM)
    o = o * lax.rsqrt(ms + DIFF_SUBLN_EPS) * gain_ref[...]
    o_ref[0] = (o * (1.0 - lambda_init)).astype(o_ref.dtype)


def _diff_attention(p_qk, p_vt, lamv, subln_g, lambda_init):
    B, S, _ = p_qk.shape
    t = DIFF_TILE
    assert S % t == 0
    W = BRANCH_WIDTH
    boff, bdiag = (jnp.asarray(b) for b in _diff_bias())
    hsum = jnp.asarray(_head_sum_mat(W, HEAD_DIM), BF16)
    gain = jnp.tile(subln_g.reshape(1, HEAD_DIM), (1, DIFF_HEADS))
    single = lambda shape: pl.BlockSpec(shape, lambda b, i: (0,) * len(shape), pipeline_mode=pl.Buffered(1))
    return pl.pallas_call(
        functools.partial(_diff_kernel, lambda_init=lambda_init),
        grid=(B, S // t),
        in_specs=[pl.BlockSpec((1, t, W), lambda b, i: (b, i, 0)), pl.BlockSpec((1, S, W), lambda b, i: (b, 0, 1)),
                  pl.BlockSpec((W, S), lambda b, i: (0, b)),
                  single(boff.shape), single(bdiag.shape), _const_spec(lamv.shape), _const_spec((1, W)),
                  _const_spec((W, W))],
        out_specs=pl.BlockSpec((1, t, W), lambda b, i: (b, i, 0)),
        out_shape=jax.ShapeDtypeStruct((B, S, W), BF16),
        scratch_shapes=[pltpu.VMEM((2 * DIFF_HEADS, 8, t), F32), pltpu.VMEM((2 * DIFF_HEADS, DIFF_ACC_ROWS, t), F32),
                        pltpu.VMEM((2, 2 * DIFF_HEADS, t, t), F32), pltpu.VMEM((2 * DIFF_HEADS, LANES, t), BF16)],
        compiler_params=_params("parallel", "arbitrary"),
        name="diffattn",
    )(p_qk, p_qk, p_vt, boff, bdiag, lamv, gain, hsum)


def _conformer_tile(p, first, u_ref, w_ref, b_ref, g_ref, beta_ref, o_ref):
    ts = p.shape[0]
    Cc = CONV_CHANNELS
    halo = CONV_HALO
    rows = CONV_ROWS

    @pl.when(first)
    def _():
        u_ref[0:halo, :] = jnp.zeros((halo, Cc), F32)

    @pl.when(jnp.logical_not(first))
    def _():
        u_ref[0:halo, :] = u_ref[ts:ts + halo, :]

    u_ref[halo:halo + ts, :] = p[:, :Cc] * _sigmoid(p[:, Cc:])
    first_tap = halo - CONV_WIDTH + 1
    for r0 in range(0, ts, rows):
        halves = []
        for lanes in (slice(0, LANES), slice(LANES, Cc)):
            acc = jnp.zeros((rows, LANES), F32) + b_ref[:, lanes]
            win = u_ref[r0:r0 + rows + halo, lanes]
            for b in range(8):
                wb = win if b == 0 else pltpu.roll(win, rows + halo - b, 0)
                for a in range(halo // 8 + 1):
                    j = 8 * a + b - first_tap
                    if 0 <= j < CONV_WIDTH:
                        acc = acc + w_ref[j:j + 1, lanes] * wb[8 * a:8 * a + rows]
            halves.append(acc)
        acc = jnp.concatenate(halves, axis=1)
        mu = jnp.mean(acc, axis=-1, keepdims=True)
        d = acc - mu
        var = jnp.mean(d * d, axis=-1, keepdims=True)
        y = d * lax.rsqrt(var + CONV_LN_EPS) * g_ref[...] + beta_ref[...]
        o_ref[r0:r0 + rows, :] = (y * _sigmoid(y)).astype(o_ref.dtype)


def _merge_ffn_kernel(x_ref, ya_ref, yb_ref, yc_ref, yd_ref, gmix_pre_ref, gmix_post_ref, wg_ref, gb_ref, wb_ref, wo_ref,
                      gffn_pre_ref, gffn_post_ref, fg_ref, fu_ref, fd_ref, out_ref):
    tm = x_ref.shape[0]
    for r0 in range(0, tm, tm // ROW_GROUPS):
        rows = slice(r0, r0 + tm // ROW_GROUPS)
        x = x_ref[rows, :]
        h = _rms(x, gmix_pre_ref[...]).astype(BF16)
        ys = (ya_ref[rows, :], yb_ref[rows, :], yc_ref[rows, :], yd_ref[rows, :])
        merged = None
        for n in range(N_BRANCHES):
            gate = _sigmoid(_dot(h, wg_ref[:, n * D_MODEL:(n + 1) * D_MODEL]) + gb_ref[n])
            term = gate * _dot(ys[n], wb_ref[n])
            merged = term if merged is None else merged + term
        x = x + _rms(_dot(merged.astype(BF16), wo_ref[...]), gmix_post_ref[...])
        h = _rms(x, gffn_pre_ref[...]).astype(BF16)
        g = _dot(h, fg_ref[...])
        u = _dot(h, fu_ref[...])
        a = (g * _sigmoid(g) * u).astype(BF16)
        out_ref[rows, :] = x + _rms(_dot(a, fd_ref[...]), gffn_post_ref[...])


def _merge_ffn(x2, ya, yb, yc, yd, gmix_pre, gmix_post, w_gate, gate_bias, w_branch, w_out,
               gffn_pre, gffn_post, f_gate, f_up, f_down):
    T = x2.shape[0]
    tm = TOKEN_TILE
    D = D_MODEL
    W = BRANCH_WIDTH
    row = lambda w: pl.BlockSpec((tm, w), lambda i: (i, 0))
    vec = _const_spec((1, D))
    single = lambda shape: pl.BlockSpec(shape, lambda i: (0,) * len(shape), pipeline_mode=pl.Buffered(1))
    return pl.pallas_call(
        _merge_ffn_kernel,
        grid=(T // tm,),
        in_specs=[row(D)] + [row(W)] * 4 + [vec, vec, single((D, N_BRANCHES * D)), _const_spec((N_BRANCHES, 1, D)),
                  single((N_BRANCHES, W, D)), single((D, D)), vec, vec, single((D, D_FF)), single((D, D_FF)),
                  single((D_FF, D))],
        out_specs=row(D),
        out_shape=jax.ShapeDtypeStruct((T, D), F32),
        compiler_params=_params("parallel"),
        name="merge_swiglu",
    )(x2, ya, yb, yc, yd, gmix_pre, gmix_post, w_gate, gate_bias, w_branch, w_out, gffn_pre, gffn_post,
      f_gate, f_up, f_down)


def _layer(x2, B, S, l, prm):
    D = D_MODEL
    W = BRANCH_WIDTH
    row = lambda a: a.reshape(1, -1)
    w_in = prm["w_in"][l]
    w_dil = w_in[:, RWKV_COLS:RWKV_COLS + DIL_COLS].reshape(D, 3, DIL_GROUPS, W)
    w_dil = jnp.concatenate([w_dil[:, 1], w_dil[:, 2], w_dil[:, 0]], axis=-1).transpose(1, 0, 2)
    diff0 = RWKV_COLS + DIL_COLS
    w_mix = jnp.concatenate([w_in[:, :RWKV_COLS]] + [w_dil[g] for g in range(DIL_GROUPS)]
                            + [w_in[:, diff0:diff0 + DIFF_QK_COLS], w_in[:, diff0 + DIFF_COLS:MIX_COLS]],
                            axis=1).astype(BF16)
    w_vt = w_in[:, diff0 + DIFF_QK_COLS:diff0 + DIFF_COLS].T.astype(BF16)
    w_gate = w_in[:, MIX_COLS:].astype(BF16)

    p_rwkv, pd0, pd1, pd2, p_qk, yd, p_vt = _inproj(
        x2, S, row(prm["norm_mix_pre"][l]), w_mix, w_vt, prm["conv_dw_w"][l].reshape(CONV_WIDTH, CONV_CHANNELS),
        row(prm["conv_dw_b"][l]), row(prm["conv_ln_g"][l]), row(prm["conv_ln_b"][l]))

    zeros = jnp.zeros((64, W), F32)
    wa_up = jnp.concatenate([jnp.concatenate([prm["rwkv_w_up"][l], zeros], axis=1),
                             jnp.concatenate([zeros, prm["rwkv_a_up"][l]], axis=1)], axis=0).astype(BF16)
    ya = _rwkv(p_rwkv.reshape(B, S, RWKV_COLS), row(prm["rwkv_mu"][l]), row(prm["rwkv_w0"][l]),
               row(prm["rwkv_a0"][l]), wa_up, prm["rwkv_g_up"][l].astype(BF16), row(prm["rwkv_k_k"][l]),
               row(prm["rwkv_k_a"][l]), row(prm["rwkv_r_k"][l]), row(prm["rwkv_ln_g"][l]),
               row(prm["rwkv_ln_b"][l]))

    group_w = DIL_COLS // DIL_GROUPS
    yb = _dilated(pd0.reshape(B, S, group_w), pd1.reshape(B, S, group_w), pd2.reshape(B, S, group_w))

    lambda_init = 0.8 - 0.6 * math.exp(-0.3 * l)
    lamv = jnp.stack([prm["diff_lam_q1"][l], prm["diff_lam_k1"][l], prm["diff_lam_q2"][l], prm["diff_lam_k2"][l]])
    yc = _diff_attention(p_qk.reshape(B, S, DIFF_QK_COLS), p_vt, lamv, prm["diff_subln_g"][l], lambda_init)

    return _merge_ffn(x2, ya.reshape(B * S, W), yb.reshape(B * S, W), yc.reshape(B * S, W), yd.reshape(B * S, W),
                      row(prm["norm_mix_pre"][l]), row(prm["norm_mix_post"][l]), w_gate,
                      prm["gate_bias"][l].reshape(N_BRANCHES, 1, D), prm["w_branch"][l].astype(BF16),
                      prm["w_out"][l].astype(BF16), row(prm["norm_ffn_pre"][l]), row(prm["norm_ffn_post"][l]),
                      prm["ffn_w_gate"][l].astype(BF16), prm["ffn_w_up"][l].astype(BF16),
                      prm["ffn_w_down"][l].astype(BF16))


def kernel(x, norm_mix_pre, norm_mix_post, norm_ffn_pre, norm_ffn_post, w_in, gate_bias, rwkv_mu, rwkv_w0, rwkv_w_up, rwkv_a0, rwkv_a_up, rwkv_g_up, rwkv_k_k, rwkv_k_a, rwkv_r_k, rwkv_ln_g, rwkv_ln_b, diff_lam_q1, diff_lam_k1, diff_lam_q2, diff_lam_k2, diff_subln_g, conv_dw_w, conv_dw_b, conv_ln_g, conv_ln_b, w_branch, w_out, ffn_w_gate, ffn_w_up, ffn_w_down):
    prm = dict(locals())
    B, S, D = x.shape
    x2 = x.reshape(B * S, D)
    for l in range(w_in.shape[0]):
        x2 = _layer(x2, B, S, l, prm)
    return x2.reshape(B, S, D)
```
